```python
import jax, jax.numpy as jnp
from jax import lax
import numpy as np

D_MODEL = 1024
BATCH = 2
SEQ = 8192
DEPTH = 1

MLA_HEADS = 8
MLA_NOPE = 128
MLA_ROPE = 64
MLA_V = 128
Q_LORA = 384
KV_LORA = 256
MOBA_HEADS = 8
MOBA_HD = 128
MOBA_BLOCK = 256
MOBA_TOPK = 3
MOBA_QCHUNK = 32
Q_BLOCK = 128
D_FF = 2816
CONV_W = 3

ROPE_THETA = 10000.0
EPS = 1e-6
NEG = -1e30
N_BRANCH = 2

OFF_QLAT = 0
OFF_KVLAT = OFF_QLAT + Q_LORA
OFF_KPE = OFF_KVLAT + KV_LORA
OFF_MOBA = OFF_KPE + MLA_ROPE
OFF_GATE = OFF_MOBA + 3 * MOBA_HEADS * MOBA_HD
IN_COLS = OFF_GATE + N_BRANCH * D_MODEL

kernel_name = 'hybrid_mla_moba_convffn_block'

F32 = jnp.float32


def rmsnorm(x, g):
    xf = x.astype(F32)
    y = xf * lax.rsqrt(jnp.mean(xf * xf, axis=-1, keepdims=True) + EPS)
    return (y * g.astype(F32)).astype(x.dtype)


def rope(x, pos):
    d = x.shape[-1]
    inv = ROPE_THETA ** (-jnp.arange(0, d, 2, dtype=F32) / d)
    ang = pos.astype(F32)[:, None] * inv[None, :]
    cos, sin = jnp.cos(ang), jnp.sin(ang)
    x1, x2 = jnp.split(x.astype(F32), 2, axis=-1)
    out = jnp.concatenate([x1 * cos - x2 * sin, x1 * sin + x2 * cos], axis=-1)
    return out.astype(x.dtype)


def causal_attention_blocks(q, k, v, scale):
    B, H, S, Dk = q.shape
    nq = S // Q_BLOCK
    qb = jnp.moveaxis(q.reshape(B, H, nq, Q_BLOCK, Dk), 2, 0)
    kpos = jnp.arange(S)

    def blk_fn(args):
        i, qi = args
        qpos = i * Q_BLOCK + jnp.arange(Q_BLOCK)
        s = jnp.einsum('bhqd,bhkd->bhqk', qi, k).astype(F32) * scale
        s = jnp.where(kpos[None, :] <= qpos[:, None], s, NEG)
        p = jax.nn.softmax(s, axis=-1).astype(v.dtype)
        return jnp.einsum('bhqk,bhkd->bhqd', p, v)

    out = lax.map(blk_fn, (jnp.arange(nq), qb))
    return jnp.moveaxis(out, 0, 2).reshape(B, H, S, v.shape[-1])


def moba_attention(q, k, v):
    B, H, S, D = q.shape
    nb = -(-S // MOBA_BLOCK)
    Sp = nb * MOBA_BLOCK
    pad = ((0, 0), (0, 0), (0, Sp - S), (0, 0))
    q, k, v = jnp.pad(q, pad), jnp.pad(k, pad), jnp.pad(v, pad)
    kb = k.reshape(B, H, nb, MOBA_BLOCK, D)
    vb = v.reshape(B, H, nb, MOBA_BLOCK, D)
    kmean = jnp.mean(kb.astype(F32), axis=3)
    gate = jnp.einsum('bhsd,bhnd->bhsn', q.astype(F32), kmean)
    q_blk = jnp.arange(Sp) // MOBA_BLOCK
    past = jnp.arange(nb)[None, :] < q_blk[:, None]
    gate = jnp.where(past, gate, -jnp.inf)
    k_sel = min(MOBA_TOPK, nb)
    top_s, top_idx = lax.top_k(gate, k_sel)
    top_ok = jnp.isfinite(top_s)

    nc = Sp // MOBA_QCHUNK

    def to_chunks(t):
        return jnp.moveaxis(t.reshape((B, H, nc, MOBA_QCHUNK) + t.shape[3:]), 2, 0)

    scale = D ** -0.5
    gather = jax.vmap(jax.vmap(lambda t, ix: t[ix]))

    def chunk_fn(args):
        c, qc, ic, okc = args
        qpos = c * MOBA_QCHUNK + jnp.arange(MOBA_QCHUNK)
        blk = (c * MOBA_QCHUNK) // MOBA_BLOCK
        ks = gather(kb, ic)
        vs = gather(vb, ic)
        ko = lax.dynamic_index_in_dim(kb, blk, axis=2, keepdims=False)
        vo = lax.dynamic_index_in_dim(vb, blk, axis=2, keepdims=False)
        s_sel = jnp.einsum('bhcd,bhckjd->bhckj', qc, ks).astype(F32) * scale
        s_sel = jnp.where(okc[..., None], s_sel, NEG)
        s_own = jnp.einsum('bhcd,bhjd->bhcj', qc, ko).astype(F32) * scale
        kpos = blk * MOBA_BLOCK + jnp.arange(MOBA_BLOCK)
        s_own = jnp.where(kpos[None, :] <= qpos[:, None], s_own, NEG)
        s = jnp.concatenate([s_sel.reshape(B, H, MOBA_QCHUNK, k_sel * MOBA_BLOCK), s_own], axis=-1)
        p = jax.nn.softmax(s, axis=-1).astype(v.dtype)
        p_sel = p[..., :k_sel * MOBA_BLOCK].reshape(B, H, MOBA_QCHUNK, k_sel, MOBA_BLOCK)
        p_own = p[..., k_sel * MOBA_BLOCK:]
        return (jnp.einsum('bhckj,bhckjd->bhcd', p_sel, vs)
                + jnp.einsum('bhcj,bhjd->bhcd', p_own, vo))

    out = lax.map(chunk_fn, (jnp.arange(nc), to_chunks(q), to_chunks(top_idx), to_chunks(top_ok)))
    return jnp.moveaxis(out, 0, 2).reshape(B, H, Sp, D)[:, :, :S]


def causal_dwconv(u, w, b):
    S = u.shape[1]
    up = jnp.pad(u, ((0, 0), (CONV_W - 1, 0), (0, 0)))
    y = b
    for j in range(CONV_W):
        y = y + w[j] * up[:, j:j + S]
    return y


def setup_inputs(seed: int = 0) -> dict:
    key = jax.random.key(seed)
    ks = jax.random.split(key, 20)
    L = DEPTH

    def w(k, shape, fan_in):
        return jax.random.normal(k, shape, F32) * fan_in ** -0.5

    def gain(k, shape):
        return 1.0 + 0.02 * jax.random.normal(k, shape, F32)

    return {
        'x': jax.random.normal(ks[0], (BATCH, SEQ, D_MODEL), F32),
        'attn_norm': gain(ks[1], (L, D_MODEL)),
        'w_in': w(ks[2], (L, D_MODEL, IN_COLS), D_MODEL),
        'b_gate': 0.02 * jax.random.normal(ks[3], (L, N_BRANCH * D_MODEL), F32),
        'q_norm': gain(ks[4], (L, Q_LORA)),
        'w_uq': w(ks[5], (L, Q_LORA, MLA_HEADS * (MLA_NOPE + MLA_ROPE)), Q_LORA),
        'kv_norm': gain(ks[6], (L, KV_LORA)),
        'w_ukv': w(ks[7], (L, KV_LORA, MLA_HEADS * (MLA_NOPE + MLA_V)), KV_LORA),
        'w_o_mla': w(ks[8], (L, MLA_HEADS * MLA_V, D_MODEL), MLA_HEADS * MLA_V),
        'w_o_moba': w(ks[9], (L, MOBA_HEADS * MOBA_HD, D_MODEL), MOBA_HEADS * MOBA_HD),
        'w_out': w(ks[10], (L, D_MODEL, D_MODEL), D_MODEL),
        'ffn_norm': gain(ks[11], (L, D_MODEL)),
        'w_up': w(ks[12], (L, D_MODEL, 2 * D_FF), D_MODEL),
        'conv_w': w(ks[13], (L, CONV_W, 2 * D_FF), CONV_W),
        'conv_b': 0.02 * jax.random.normal(ks[14], (L, 2 * D_FF), F32),
        'w_down': w(ks[15], (L, D_FF, D_MODEL), D_FF),
        'final_norm': gain(ks[16], (D_MODEL,)),
    }


def reference(x, attn_norm, w_in, b_gate, q_norm, w_uq, kv_norm, w_ukv, w_o_mla, w_o_moba,
              w_out, ffn_norm, w_up, conv_w, conv_b, w_down, final_norm):
    B, S, _ = x.shape
    pos = jnp.arange(S)
    for l in range(DEPTH):
        h = rmsnorm(x, attn_norm[l])
        proj = h @ w_in[l]
        q_lat = proj[..., OFF_QLAT:OFF_KVLAT]
        kv_lat = proj[..., OFF_KVLAT:OFF_KPE]
        k_pe = proj[..., OFF_KPE:OFF_MOBA]
        qkv_b = proj[..., OFF_MOBA:OFF_GATE]
        gates = jax.nn.sigmoid(proj[..., OFF_GATE:] + b_gate[l])
        g_a, g_b = gates[..., :D_MODEL], gates[..., D_MODEL:]

        q = (rmsnorm(q_lat, q_norm[l]) @ w_uq[l]).reshape(B, S, MLA_HEADS, MLA_NOPE + MLA_ROPE)
        q = q.transpose(0, 2, 1, 3)
        kv = (rmsnorm(kv_lat, kv_norm[l]) @ w_ukv[l]).reshape(B, S, MLA_HEADS, MLA_NOPE + MLA_V)
        kv = kv.transpose(0, 2, 1, 3)
        q_a = jnp.concatenate([q[..., :MLA_NOPE], rope(q[..., MLA_NOPE:], pos)], axis=-1)
        k_rot = jnp.broadcast_to(rope(k_pe[:, None], pos), (B, MLA_HEADS, S, MLA_ROPE))
        k_a = jnp.concatenate([kv[..., :MLA_NOPE], k_rot], axis=-1)
        v_a = kv[..., MLA_NOPE:]
        y_a = causal_attention_blocks(q_a, k_a, v_a, (MLA_NOPE + MLA_ROPE) ** -0.5)
        y_a = y_a.transpose(0, 2, 1, 3).reshape(B, S, MLA_HEADS * MLA_V) @ w_o_mla[l]

        qkv = qkv_b.reshape(B, S, 3, MOBA_HEADS, MOBA_HD).transpose(2, 0, 3, 1, 4)
        y_b = moba_attention(rope(qkv[0], pos), rope(qkv[1], pos), qkv[2])
        y_b = y_b.transpose(0, 2, 1, 3).reshape(B, S, MOBA_HEADS * MOBA_HD) @ w_o_moba[l]

        x = x + (g_a * y_a + g_b * y_b) @ w_out[l]

        h = rmsnorm(x, ffn_norm[l])
        u = causal_dwconv(h @ w_up[l], conv_w[l], conv_b[l])
        x = x + (jax.nn.silu(u[..., :D_FF]) * u[..., D_FF:]) @ w_down[l]
    return rmsnorm(x, final_norm)
```

```python
import functools

import jax
import jax.numpy as jnp
from jax import lax
from jax.experimental import pallas as pl
from jax.experimental.pallas import tpu as pltpu

F32 = jnp.float32
BF16 = jnp.bfloat16

D_MODEL = 1024
MLA_HEADS = 8
MLA_NOPE = 128
MLA_ROPE = 64
MLA_V = 128
Q_LORA = 384
KV_LORA = 256
MOBA_HEADS = 8
MOBA_HD = 128
MOBA_BLOCK = 256
MOBA_TOPK = 3
D_FF = 2816
CONV_W = 3
ROPE_THETA = 10000.0
EPS = 1e-6
NEG = -1e30
N_BRANCH = 2

OFF_QLAT = 0
OFF_KVLAT = OFF_QLAT + Q_LORA
OFF_KPE = OFF_KVLAT + KV_LORA
OFF_MOBA = OFF_KPE + MLA_ROPE
OFF_GATE = OFF_MOBA + 3 * MOBA_HEADS * MOBA_HD

LANES = 128
MLA_QK = 2 * LANES
VMEM_LIMIT = 56 * 1024 * 1024

TM_PROJ = 256
TQ_MLA = 512
TQ_MOBA = 512
TM_MIX = 512
TM_FFN = 512
FF_CHUNK = 256


def _dot(a, b):
    return jnp.dot(a, b, preferred_element_type=F32)


def _dot_nt(a, b):
    return lax.dot_general(a, b, (((1,), (1,)), ((), ())), preferred_element_type=F32)


def _rms(x, g):
    return x * lax.rsqrt(jnp.mean(x * x, axis=-1, keepdims=True) + EPS) * g


def _const_spec(shape):
    n = len(shape)
    return pl.BlockSpec(shape, lambda *_: (0,) * n, pipeline_mode=pl.Buffered(1))


def _proj_kernel(x_ref, an_ref, wq_ref, wkv_ref, wpe_ref, wmq_ref, wmk_ref, wmv_ref, wg_ref,
                 bg_ref, qn_ref, wuqn_ref, wuqr_ref, kvn_ref, wuk_ref, wuv_ref,
                 cosa_ref, sina_ref, cosb_ref, sinb_ref,
                 qa_ref, ka_ref, va_ref, qb_ref, kb_ref, vb_ref, g_ref):
    hb = _rms(x_ref[0], an_ref[...]).astype(BF16)
    lane = lax.broadcasted_iota(jnp.int32, (hb.shape[0], LANES), 1)
    low_half = lane < MLA_ROPE
    first32 = (lane % MLA_ROPE) < (MLA_ROPE // 2)
    cosa, sina = cosa_ref[...], sina_ref[...]
    cosb, sinb = cosb_ref[...], sinb_ref[...]

    def rope64(c):
        partner = jnp.where(first32, pltpu.roll(c, LANES - 32, 1), pltpu.roll(c, 32, 1))
        return c * cosa + partner * sina

    def rope128(c):
        return c * cosb + pltpu.roll(c, LANES // 2, 1) * sinb

    ql = _rms(_dot(hb, wq_ref[...]), qn_ref[...]).astype(BF16)
    scale_a = (MLA_NOPE + MLA_ROPE) ** -0.5
    qn = _dot(ql, wuqn_ref[...]) * scale_a
    qr = _dot(ql, wuqr_ref[...]) * scale_a
    for h in range(MLA_HEADS):
        qa_ref[0, h, :, 0:LANES] = qn[:, h * LANES:(h + 1) * LANES].astype(BF16)
    for c in range(MLA_HEADS // 2):
        r = rope64(qr[:, c * LANES:(c + 1) * LANES])
        qa_ref[0, 2 * c, :, LANES:] = jnp.where(low_half, r, 0.0).astype(BF16)
        qa_ref[0, 2 * c + 1, :, LANES:] = jnp.where(
            low_half, pltpu.roll(r, LANES // 2, 1), 0.0).astype(BF16)

    kvl = _rms(_dot(hb, wkv_ref[...]), kvn_ref[...]).astype(BF16)
    kn = _dot(kvl, wuk_ref[...])
    vv = _dot(kvl, wuv_ref[...])
    krot = rope64(_dot(hb, wpe_ref[...])).astype(BF16)
    for h in range(MLA_HEADS):
        ka_ref[0, h, :, 0:LANES] = kn[:, h * LANES:(h + 1) * LANES].astype(BF16)
        ka_ref[0, h, :, LANES:] = krot
        va_ref[0, h] = vv[:, h * LANES:(h + 1) * LANES].astype(BF16)

    mq = _dot(hb, wmq_ref[...])
    for h in range(MOBA_HEADS):
        qb_ref[0, h] = rope128(mq[:, h * LANES:(h + 1) * LANES])
    mk = _dot(hb, wmk_ref[...])
    for h in range(MOBA_HEADS):
        kb_ref[0, h] = rope128(mk[:, h * LANES:(h + 1) * LANES])
    mv = _dot(hb, wmv_ref[...])
    for h in range(MOBA_HEADS):
        vb_ref[0, h] = mv[:, h * LANES:(h + 1) * LANES].astype(BF16)

    z = _dot(hb, wg_ref[...]) + bg_ref[...]
    g_ref[0] = (1.0 / (1.0 + jnp.exp(-z))).astype(BF16)


def _mla_kernel(q_ref, k_ref, v_ref, o_ref, *, tq):
    qi = pl.program_id(2)
    q = q_ref[0, 0]
    row = lax.broadcasted_iota(jnp.int32, (tq, tq), 0)
    col = lax.broadcasted_iota(jnp.int32, (tq, tq), 1)

    start = pl.multiple_of(qi * tq, tq)
    s = _dot_nt(q, k_ref[0, 0, pl.ds(start, tq), :])
    s = jnp.where(col <= row, s, NEG)
    m = jnp.max(s, axis=-1, keepdims=True)
    p = jnp.exp(s - m)
    l = jnp.sum(p, axis=-1, keepdims=True)
    acc = _dot(p.astype(BF16), v_ref[0, 0, pl.ds(start, tq), :])

    def body(j, carry):
        m, l, acc = carry
        off = pl.multiple_of(j * tq, tq)
        s = _dot_nt(q, k_ref[0, 0, pl.ds(off, tq), :])
        m_new = jnp.maximum(m, jnp.max(s, axis=-1, keepdims=True))
        alpha = jnp.exp(m - m_new)
        p = jnp.exp(s - m_new)
        l = alpha * l + jnp.sum(p, axis=-1, keepdims=True)
        acc = alpha * acc + _dot(p.astype(BF16), v_ref[0, 0, pl.ds(off, tq), :])
        return m_new, l, acc

    m, l, acc = lax.fori_loop(0, qi, body, (m, l, acc))
    o_ref[0] = (acc / l).astype(BF16)


def _moba_kernel(q_ref, k_ref, v_ref, o_ref, kaug_ref, kmean_ref, qaug_ref, *, tq, nb):
    qi = pl.program_id(2)

    @pl.when(qi == 0)
    def _():
        lane = lax.broadcasted_iota(jnp.int32, (MOBA_BLOCK, LANES), 1)

        def blk_body(b, c):
            off = pl.multiple_of(b * MOBA_BLOCK, MOBA_BLOCK)
            kb = k_ref[0, 0, pl.ds(off, MOBA_BLOCK), :]
            kmean_ref[pl.ds(b, 1), :] = jnp.mean(kb, axis=0, keepdims=True)
            kaug_ref[pl.ds(off, MOBA_BLOCK), 0:LANES] = kb.astype(BF16)
            kaug_ref[pl.ds(off, MOBA_BLOCK), LANES:] = jnp.where(lane == b, 1.0, 0.0).astype(BF16)
            return c

        lax.fori_loop(0, nb, blk_body, 0)

    q = q_ref[0, 0]

    gt = lax.dot_general(kmean_ref[...], q, (((1,), (1,)), ((), ())),
                         precision=lax.Precision.HIGHEST, preferred_element_type=F32)
    kblk = lax.broadcasted_iota(jnp.int32, (nb, tq), 0)
    qblk = qi * (tq // MOBA_BLOCK) + lax.broadcasted_iota(jnp.int32, (nb, tq), 1) // MOBA_BLOCK
    past = kblk < qblk
    g = jnp.where(past, gt, -jnp.inf)
    cnt = jnp.zeros((nb, tq), F32)
    for jp in range(nb):
        gj = g[jp:jp + 1, :]
        beats = (gj > g) | ((gj == g) & (jp < kblk))
        cnt = cnt + jnp.where(beats, 1.0, 0.0)
    keep = (past & (cnt < float(MOBA_TOPK))) | (kblk == qblk)
    bias_t = jnp.where(keep, 0.0, NEG)
    bias_t = jnp.concatenate([bias_t, jnp.zeros((LANES - nb, tq), F32)], axis=0)
    qaug_ref[:, 0:LANES] = (q * (MOBA_HD ** -0.5)).astype(BF16)
    qaug_ref[:, LANES:] = bias_t.T.astype(BF16)
    qa = qaug_ref[...]

    row = lax.broadcasted_iota(jnp.int32, (tq, tq), 0)
    col = lax.broadcasted_iota(jnp.int32, (tq, tq), 1)
    start = pl.multiple_of(qi * tq, tq)
    s = _dot_nt(qa, kaug_ref[pl.ds(start, tq), :])
    s = jnp.where(col <= row, s, NEG)
    m = jnp.max(s, axis=-1, keepdims=True)
    p = jnp.exp(s - m)
    l = jnp.sum(p, axis=-1, keepdims=True)
    acc = _dot(p.astype(BF16), v_ref[0, 0, pl.ds(start, tq), :])

    def body(j, carry):
        m, l, acc = carry
        off = pl.multiple_of(j * tq, tq)
        s = _dot_nt(qa, kaug_ref[pl.ds(off, tq), :])
        m_new = jnp.maximum(m, jnp.max(s, axis=-1, keepdims=True))
        alpha = jnp.exp(m - m_new)
        p = jnp.exp(s - m_new)
        l = alpha * l + jnp.sum(p, axis=-1, keepdims=True)
        acc = alpha * acc + _dot(p.astype(BF16), v_ref[0, 0, pl.ds(off, tq), :])
        return m_new, l, acc

    m, l, acc = lax.fori_loop(0, qi, body, (m, l, acc))
    o_ref[0] = (acc / l).astype(BF16)


def _mix_kernel(x_ref, ya_ref, yb_ref, g_ref, woa_ref, wob_ref, wout_ref, o_ref):
    a = _dot(ya_ref[...], woa_ref[...])
    b = _dot(yb_ref[...], wob_ref[...])
    g = g_ref[...].astype(F32)
    mix = g[:, :D_MODEL] * a + g[:, D_MODEL:] * b
    o_ref[...] = x_ref[...] + _dot(mix.astype(BF16), wout_ref[...])


def _ffn_kernel(x_ref, fn_ref, wup_ref, cw_ref, cb_ref, wdn_ref, gn_ref, o_ref, carry_ref,
                *, tm, n_chunks):
    si = pl.program_id(1)

    @pl.when(si == 0)
    def _():
        carry_ref[...] = jnp.zeros_like(carry_ref)

    x = x_ref[0]
    hb = _rms(x, fn_ref[...]).astype(BF16)
    row = lax.broadcasted_iota(jnp.int32, (tm, FF_CHUNK), 0)

    def conv(u, idx, col0):
        prev = carry_ref[idx]
        u1 = jnp.where(row == 0, prev[7:8, :], pltpu.roll(u, 1, 0))
        u2 = jnp.where(row == 0, prev[6:7, :],
                       jnp.where(row == 1, prev[7:8, :], pltpu.roll(u, 2, 0)))
        carry_ref[idx] = u[tm - 8:, :]
        w = cw_ref[:, col0:col0 + FF_CHUNK]
        return cb_ref[:, col0:col0 + FF_CHUNK] + w[0:1] * u2 + w[1:2] * u1 + w[2:3] * u

    acc = jnp.zeros((tm, D_MODEL), F32)
    for c in range(n_chunks):
        ca = c * FF_CHUNK
        cb = D_FF + c * FF_CHUNK
        ya = conv(_dot(hb, wup_ref[:, ca:ca + FF_CHUNK]), c, ca)
        yb = conv(_dot(hb, wup_ref[:, cb:cb + FF_CHUNK]), n_chunks + c, cb)
        act = ya * (1.0 / (1.0 + jnp.exp(-ya))) * yb
        acc = acc + _dot(act.astype(BF16), wdn_ref[ca:ca + FF_CHUNK, :])
    o_ref[0] = _rms(x + acc, gn_ref[...])


def _rope_tables(seq, d):
    inv = ROPE_THETA ** (-jnp.arange(0, d, 2, dtype=F32) / d)
    ang = jnp.arange(seq).astype(F32)[:, None] * inv[None, :]
    cos, sin = jnp.cos(ang), jnp.sin(ang)
    reps = LANES // d
    return (jnp.tile(jnp.concatenate([cos, cos], axis=-1), (1, reps)),
            jnp.tile(jnp.concatenate([-sin, sin], axis=-1), (1, reps)))


def _params(n_axes):
    return pltpu.CompilerParams(dimension_semantics=("arbitrary",) * n_axes,
                                vmem_limit_bytes=VMEM_LIMIT)


def _layer(x, attn_norm, w_in, b_gate, q_norm, w_uq, kv_norm, w_ukv, w_o_mla, w_o_moba,
           w_out, ffn_norm, w_up, conv_w, conv_b, w_down, out_gain):
    B, S, D = x.shape
    H = MLA_HEADS
    assert D == D_MODEL and S % TQ_MLA == 0 and S % TQ_MOBA == 0 and S % MOBA_BLOCK == 0
    assert D_FF % FF_CHUNK == 0 and TQ_MOBA % MOBA_BLOCK == 0
    nb = S // MOBA_BLOCK
    assert nb <= LANES

    wb = w_in.astype(BF16)
    wq = wb[:, OFF_QLAT:OFF_KVLAT]
    wkv = wb[:, OFF_KVLAT:OFF_KPE]
    wpe = jnp.pad(wb[:, OFF_KPE:OFF_MOBA], ((0, 0), (0, LANES - MLA_ROPE)))
    hd = MOBA_HEADS * MOBA_HD
    wmq = wb[:, OFF_MOBA:OFF_MOBA + hd]
    wmk = wb[:, OFF_MOBA + hd:OFF_MOBA + 2 * hd]
    wmv = wb[:, OFF_MOBA + 2 * hd:OFF_GATE]
    wg = wb[:, OFF_GATE:]
    wuq3 = w_uq.astype(BF16).reshape(Q_LORA, H, MLA_NOPE + MLA_ROPE)
    wuqn = wuq3[:, :, :MLA_NOPE].reshape(Q_LORA, H * MLA_NOPE)
    wuqr = wuq3[:, :, MLA_NOPE:].reshape(Q_LORA, H * MLA_ROPE)
    wukv3 = w_ukv.astype(BF16).reshape(KV_LORA, H, MLA_NOPE + MLA_V)
    wuk = wukv3[:, :, :MLA_NOPE].reshape(KV_LORA, H * MLA_NOPE)
    wuv = wukv3[:, :, MLA_NOPE:].reshape(KV_LORA, H * MLA_V)
    cosa, sina = _rope_tables(S, MLA_ROPE)
    cosb, sinb = _rope_tables(S, MOBA_HD)

    row2 = lambda v: v.reshape(1, -1)
    tm = TM_PROJ
    head_out = lambda w, dt: jax.ShapeDtypeStruct((B, H, S, w), dt)
    head_spec = lambda w: pl.BlockSpec((1, H, tm, w), lambda b, s: (b, 0, s, 0))
    tab_spec = pl.BlockSpec((tm, LANES), lambda b, s: (s, 0))
    weights = [row2(attn_norm), wq, wkv, wpe, wmq, wmk, wmv, wg, row2(b_gate), row2(q_norm),
               wuqn, wuqr, row2(kv_norm), wuk, wuv]
    qa, ka, va, qb, kb, vb, g = pl.pallas_call(
        _proj_kernel,
        grid=(B, S // tm),
        in_specs=[pl.BlockSpec((1, tm, D), lambda b, s: (b, s, 0))]
        + [_const_spec(w.shape) for w in weights] + [tab_spec] * 4,
        out_specs=[head_spec(MLA_QK), head_spec(MLA_QK), head_spec(MLA_V),
                   head_spec(MOBA_HD), head_spec(MOBA_HD), head_spec(MOBA_HD),
                   pl.BlockSpec((1, tm, N_BRANCH * D), lambda b, s: (b, s, 0))],
        out_shape=[head_out(MLA_QK, BF16), head_out(MLA_QK, BF16), head_out(MLA_V, BF16),
                   head_out(MOBA_HD, F32), head_out(MOBA_HD, F32), head_out(MOBA_HD, BF16),
                   jax.ShapeDtypeStruct((B, S, N_BRANCH * D), BF16)],
        compiler_params=_params(2),
        name="proj",
    )(x, *weights, cosa, sina, cosb, sinb)

    tq = TQ_MLA
    ya = pl.pallas_call(
        functools.partial(_mla_kernel, tq=tq),
        grid=(B, H, S // tq),
        in_specs=[pl.BlockSpec((1, 1, tq, MLA_QK), lambda b, h, i: (b, h, i, 0)),
                  pl.BlockSpec((1, 1, S, MLA_QK), lambda b, h, i: (b, h, 0, 0)),
                  pl.BlockSpec((1, 1, S, MLA_V), lambda b, h, i: (b, h, 0, 0))],
        out_specs=pl.BlockSpec((1, tq, MLA_V), lambda b, h, i: (b, i, h)),
        out_shape=jax.ShapeDtypeStruct((B, S, H * MLA_V), BF16),
        compiler_params=_params(3),
        name="mla_attn",
    )(qa, ka, va)

    tq = TQ_MOBA
    yb = pl.pallas_call(
        functools.partial(_moba_kernel, tq=tq, nb=nb),
        grid=(B, H, S // tq),
        in_specs=[pl.BlockSpec((1, 1, tq, MOBA_HD), lambda b, h, i: (b, h, i, 0)),
                  pl.BlockSpec((1, 1, S, MOBA_HD), lambda b, h, i: (b, h, 0, 0)),
                  pl.BlockSpec((1, 1, S, MOBA_HD), lambda b, h, i: (b, h, 0, 0))],
        out_specs=pl.BlockSpec((1, tq, MOBA_HD), lambda b, h, i: (b, i, h)),
        out_shape=jax.ShapeDtypeStruct((B, S, H * MOBA_HD), BF16),
        scratch_shapes=[pltpu.VMEM((S, 2 * LANES), BF16),
                        pltpu.VMEM((nb, LANES), F32),
                        pltpu.VMEM((tq, 2 * LANES), BF16)],
        compiler_params=_params(3),
        name="moba_attn",
    )(qb, kb, vb)

    tm = TM_MIX
    rows = lambda w: pl.BlockSpec((tm, w), lambda i: (i, 0))
    mix_w = [w_o_mla.astype(BF16), w_o_moba.astype(BF16), w_out.astype(BF16)]
    x1 = pl.pallas_call(
        _mix_kernel,
        grid=(B * S // tm,),
        in_specs=[rows(D), rows(D), rows(D), rows(N_BRANCH * D)]
        + [_const_spec(w.shape) for w in mix_w],
        out_specs=rows(D),
        out_shape=jax.ShapeDtypeStruct((B * S, D), F32),
        compiler_params=_params(1),
        name="mix",
    )(x.reshape(B * S, D), ya.reshape(B * S, D), yb.reshape(B * S, D),
      g.reshape(B * S, N_BRANCH * D), *mix_w)

    tm = TM_FFN
    n_chunks = D_FF // FF_CHUNK
    ffn_w = [row2(ffn_norm), w_up.astype(BF16), conv_w, row2(conv_b), w_down.astype(BF16),
             row2(out_gain)]
    out = pl.pallas_call(
        functools.partial(_ffn_kernel, tm=tm, n_chunks=n_chunks),
        grid=(B, S // tm),
        in_specs=[pl.BlockSpec((1, tm, D), lambda b, s: (b, s, 0))]
        + [_const_spec(w.shape) for w in ffn_w],
        out_specs=pl.BlockSpec((1, tm, D), lambda b, s: (b, s, 0)),
        out_shape=jax.ShapeDtypeStruct((B, S, D), F32),
        scratch_shapes=[pltpu.VMEM((2 * n_chunks, 8, FF_CHUNK), F32)],
        compiler_params=_params(2),
        name="ffn",
    )(x1.reshape(B, S, D), *ffn_w)
    return out


def kernel(x, attn_norm, w_in, b_gate, q_norm, w_uq, kv_norm, w_ukv, w_o_mla, w_o_moba,
           w_out, ffn_norm, w_up, conv_w, conv_b, w_down, final_norm):
    depth = w_in.shape[0]
    assert depth == 1
    return _layer(x, attn_norm[0], w_in[0], b_gate[0], q_norm[0], w_uq[0], kv_norm[0],
                  w_ukv[0], w_o_mla[0], w_o_moba[0], w_out[0], ffn_norm[0], w_up[0],
                  conv_w[0], conv_b[0], w_down[0], final_norm)
```

```python
import functools

import jax
import jax.numpy as jnp
from jax import lax
from jax.experimental import pallas as pl
from jax.experimental.pallas import tpu as pltpu

F32 = jnp.float32
BF16 = jnp.bfloat16

D_MODEL = 1024
MLA_HEADS = 8
MLA_NOPE = 128
MLA_ROPE = 64
MLA_V = 128
Q_LORA = 384
KV_LORA = 256
MOBA_HEADS = 8
MOBA_HD = 128
MOBA_BLOCK = 256
MOBA_TOPK = 3
D_FF = 2816
CONV_W = 3
ROPE_THETA = 10000.0
EPS = 1e-6
NEG = -1e30
N_BRANCH = 2
LOG2E = 1.4426950408889634

OFF_QLAT = 0
OFF_KVLAT = OFF_QLAT + Q_LORA
OFF_KPE = OFF_KVLAT + KV_LORA
OFF_MOBA = OFF_KPE + MLA_ROPE
OFF_GATE = OFF_MOBA + 3 * MOBA_HEADS * MOBA_HD

LANES = 128
MLA_QK = 2 * LANES
VMEM_LIMIT = 56 * 1024 * 1024

TM_PROJ = 256
TQ_MLA = 512
TQ_MOBA = 512
TM_MIX = 512
TM_FFN = 512
FF_CHUNK = 256


def _dot(a, b):
    return jnp.dot(a, b, preferred_element_type=F32)


def _dot_nt(a, b):
    return lax.dot_general(a, b, (((1,), (1,)), ((), ())), preferred_element_type=F32)


def _rms(x, g):
    return x * lax.rsqrt(jnp.mean(x * x, axis=-1, keepdims=True) + EPS) * g


def _const_spec(shape):
    n = len(shape)
    return pl.BlockSpec(shape, lambda *_: (0,) * n, pipeline_mode=pl.Buffered(1))


def _proj_kernel(x_ref, an_ref, wq_ref, wkv_ref, wpe_ref, wmq_ref, wmk_ref, wmv_ref, wg_ref,
                 bg_ref, qn_ref, wuqn_ref, wuqr_ref, kvn_ref, wuk_ref, wuv_ref,
                 cosa_ref, sina_ref, cosb_ref, sinb_ref,
                 qa_ref, ka_ref, va_ref, qb_ref, kb_ref, vb_ref, g_ref):
    hb = _rms(x_ref[0], an_ref[...]).astype(BF16)
    lane = lax.broadcasted_iota(jnp.int32, (hb.shape[0], LANES), 1)
    low_half = lane < MLA_ROPE
    first32 = (lane % MLA_ROPE) < (MLA_ROPE // 2)
    cosa, sina = cosa_ref[...], sina_ref[...]
    cosb, sinb = cosb_ref[...], sinb_ref[...]

    def rope64(c):
        partner = jnp.where(first32, pltpu.roll(c, LANES - 32, 1), pltpu.roll(c, 32, 1))
        return c * cosa + partner * sina

    def rope128(c):
        return c * cosb + pltpu.roll(c, LANES // 2, 1) * sinb

    ql = _rms(_dot(hb, wq_ref[...]), qn_ref[...]).astype(BF16)
    scale_a = (MLA_NOPE + MLA_ROPE) ** -0.5 * LOG2E
    qn = _dot(ql, wuqn_ref[...]) * scale_a
    qr = _dot(ql, wuqr_ref[...]) * scale_a
    for h in range(MLA_HEADS):
        qa_ref[0, h, :, 0:LANES] = qn[:, h * LANES:(h + 1) * LANES].astype(BF16)
    for c in range(MLA_HEADS // 2):
        r = rope64(qr[:, c * LANES:(c + 1) * LANES])
        qa_ref[0, 2 * c, :, LANES:] = jnp.where(low_half, r, 0.0).astype(BF16)
        qa_ref[0, 2 * c + 1, :, LANES:] = jnp.where(
            low_half, pltpu.roll(r, LANES // 2, 1), 0.0).astype(BF16)

    kvl = _rms(_dot(hb, wkv_ref[...]), kvn_ref[...]).astype(BF16)
    kn = _dot(kvl, wuk_ref[...])
    vv = _dot(kvl, wuv_ref[...])
    krot = rope64(_dot(hb, wpe_ref[...])).astype(BF16)
    for h in range(MLA_HEADS):
        ka_ref[0, h, :, 0:LANES] = kn[:, h * LANES:(h + 1) * LANES].astype(BF16)
        ka_ref[0, h, :, LANES:] = krot
        va_ref[0, h] = vv[:, h * LANES:(h + 1) * LANES].astype(BF16)

    mq = _dot(hb, wmq_ref[...])
    for h in range(MOBA_HEADS):
        qb_ref[0, h] = rope128(mq[:, h * LANES:(h + 1) * LANES])
    mk = _dot(hb, wmk_ref[...])
    for h in range(MOBA_HEADS):
        kb_ref[0, h] = rope128(mk[:, h * LANES:(h + 1) * LANES])
    mv = _dot(hb, wmv_ref[...])
    for h in range(MOBA_HEADS):
        vb_ref[0, h] = mv[:, h * LANES:(h + 1) * LANES].astype(BF16)

    z = _dot(hb, wg_ref[...]) + bg_ref[...]
    g_ref[0] = (1.0 / (1.0 + jnp.exp(-z))).astype(BF16)


def _flash_causal(qi, q, k_tile, v_tile, o_ref, scr, *, tq):
    s_refs, x_refs, p_refs = scr[0:2], scr[2:4], scr[4:6]
    m_ref, l_ref, acc_ref = scr[6:9]
    nc = tq // LANES
    chunk = lambda c: slice(c * LANES, (c + 1) * LANES)

    def scores(i, slot, masked):
        s = _dot_nt(q, k_tile(i))
        if masked:
            row = lax.broadcasted_iota(jnp.int32, (tq, tq), 0)
            col = lax.broadcasted_iota(jnp.int32, (tq, tq), 1)
            s = jnp.where(col <= row, s, NEG)
        s_refs[slot][...] = s
        mx = s[:, chunk(0)]
        for c in range(1, nc):
            mx = jnp.maximum(mx, s[:, chunk(c)])
        x_refs[slot][...] = jnp.broadcast_to(jnp.max(mx, axis=-1, keepdims=True), (tq, LANES))

    def softmax(slot, first):
        if first:
            m_new, alpha = x_refs[slot][...], None
        else:
            m_old = m_ref[...]
            m_new = jnp.maximum(m_old, x_refs[slot][...])
            alpha = jnp.exp2(m_old - m_new)
        psum = None
        for c in range(nc):
            pc = jnp.exp2(s_refs[slot][:, chunk(c)] - m_new)
            p_refs[slot][:, chunk(c)] = pc.astype(BF16)
            psum = pc if psum is None else psum + pc
        m_ref[...] = m_new
        l_ref[...] = psum if first else alpha * l_ref[...] + psum
        return alpha

    scores(qi, 0, True)
    scores(0, 1, False)
    softmax(0, True)
    acc_ref[...] = jnp.zeros_like(acc_ref)

    def step(t, cur, prv):
        scores(jnp.minimum(t, qi - 1), prv, False)
        pv = _dot(p_refs[prv][...], v_tile(jnp.where(t == 1, qi, t - 2)))
        alpha = softmax(cur, False)
        acc_ref[...] = (acc_ref[...] + pv) * alpha

    def pair(u, c):
        step(2 * u + 1, 1, 0)
        step(2 * u + 2, 0, 1)
        return c

    lax.fori_loop(0, qi // 2, pair, 0)

    @pl.when(qi % 2 == 1)
    def _():
        step(qi, 1, 0)

    def finish(slot):
        pv = _dot(p_refs[slot][...], v_tile(jnp.maximum(qi - 1, 0)))
        l = jnp.sum(l_ref[...], axis=-1, keepdims=True)
        o_ref[0] = ((acc_ref[...] + pv) / l).astype(BF16)

    @pl.when(qi % 2 == 0)
    def _():
        finish(0)

    @pl.when(qi % 2 == 1)
    def _():
        finish(1)


def _flash_scratch(tq, dv):
    assert dv == LANES
    return ([pltpu.VMEM((tq, tq), F32)] * 2 + [pltpu.VMEM((tq, LANES), F32)] * 2
            + [pltpu.VMEM((tq, tq), BF16)] * 2 + [pltpu.VMEM((tq, LANES), F32)] * 3)


def _mla_kernel(q_ref, k_ref, v_ref, o_ref, *scr, tq):
    qi = pl.program_id(2)
    tile = lambda ref: lambda i: ref[0, 0, pl.ds(pl.multiple_of(i * tq, tq), tq), :]
    _flash_causal(qi, q_ref[0, 0], tile(k_ref), tile(v_ref), o_ref, scr, tq=tq)


def _moba_kernel(q_ref, k_ref, v_ref, o_ref, kaug_ref, kmean_ref, qaug_ref, *scr, tq, nb):
    qi = pl.program_id(2)

    @pl.when(qi == 0)
    def _():
        lane = lax.broadcasted_iota(jnp.int32, (MOBA_BLOCK, LANES), 1)

        def blk_body(b, c):
            off = pl.multiple_of(b * MOBA_BLOCK, MOBA_BLOCK)
            kb = k_ref[0, 0, pl.ds(off, MOBA_BLOCK), :]
            kmean_ref[pl.ds(b, 1), :] = jnp.mean(kb, axis=0, keepdims=True)
            kaug_ref[pl.ds(off, MOBA_BLOCK), 0:LANES] = kb.astype(BF16)
            kaug_ref[pl.ds(off, MOBA_BLOCK), LANES:] = jnp.where(lane == b, 1.0, 0.0).astype(BF16)
            return c

        lax.fori_loop(0, nb, blk_body, 0)

    q = q_ref[0, 0]

    gt = lax.dot_general(kmean_ref[...], q, (((1,), (1,)), ((), ())),
                         precision=lax.Precision.HIGHEST, preferred_element_type=F32)
    kblk = lax.broadcasted_iota(jnp.int32, (nb, tq), 0)
    qblk = qi * (tq // MOBA_BLOCK) + lax.broadcasted_iota(jnp.int32, (nb, tq), 1) // MOBA_BLOCK
    past = kblk < qblk
    g = jnp.where(past, gt, -jnp.inf)
    cnt = jnp.zeros((nb, tq), F32)
    for jp in range(nb):
        gj = g[jp:jp + 1, :]
        beats = (gj > g) | ((gj == g) & (jp < kblk))
        cnt = cnt + jnp.where(beats, 1.0, 0.0)
    keep = (past & (cnt < float(MOBA_TOPK))) | (kblk == qblk)
    bias_t = jnp.where(keep, 0.0, NEG)
    bias_t = jnp.concatenate([bias_t, jnp.zeros((LANES - nb, tq), F32)], axis=0)
    qaug_ref[:, 0:LANES] = (q * (MOBA_HD ** -0.5 * LOG2E)).astype(BF16)
    qaug_ref[:, LANES:] = bias_t.T.astype(BF16)

    k_tile = lambda i: kaug_ref[pl.ds(pl.multiple_of(i * tq, tq), tq), :]
    v_tile = lambda i: v_ref[0, 0, pl.ds(pl.multiple_of(i * tq, tq), tq), :]
    _flash_causal(qi, qaug_ref[...], k_tile, v_tile, o_ref, scr, tq=tq)


def _mix_kernel(x_ref, ya_ref, yb_ref, g_ref, woa_ref, wob_ref, wout_ref, o_ref):
    a = _dot(ya_ref[...], woa_ref[...])
    b = _dot(yb_ref[...], wob_ref[...])
    g = g_ref[...].astype(F32)
    mix = g[:, :D_MODEL] * a + g[:, D_MODEL:] * b
    o_ref[...] = x_ref[...] + _dot(mix.astype(BF16), wout_ref[...])


def _ffn_kernel(x_ref, fn_ref, wup_ref, cw_ref, cb_ref, wdn_ref, gn_ref, o_ref, carry_ref,
                *, tm, n_chunks):
    si = pl.program_id(1)

    @pl.when(si == 0)
    def _():
        carry_ref[...] = jnp.zeros_like(carry_ref)

    x = x_ref[0]
    hb = _rms(x, fn_ref[...]).astype(BF16)
    row = lax.broadcasted_iota(jnp.int32, (tm, FF_CHUNK), 0)

    def conv(u, idx, col0):
        prev = carry_ref[idx]
        u1 = jnp.where(row == 0, prev[7:8, :], pltpu.roll(u, 1, 0))
        u2 = jnp.where(row == 0, prev[6:7, :],
                       jnp.where(row == 1, prev[7:8, :], pltpu.roll(u, 2, 0)))
        carry_ref[idx] = u[tm - 8:, :]
        w = cw_ref[:, col0:col0 + FF_CHUNK]
        return cb_ref[:, col0:col0 + FF_CHUNK] + w[0:1] * u2 + w[1:2] * u1 + w[2:3] * u

    acc = jnp.zeros((tm, D_MODEL), F32)
    for c in range(n_chunks):
        ca = c * FF_CHUNK
        cb = D_FF + c * FF_CHUNK
        ya = conv(_dot(hb, wup_ref[:, ca:ca + FF_CHUNK]), c, ca)
        yb = conv(_dot(hb, wup_ref[:, cb:cb + FF_CHUNK]), n_chunks + c, cb)
        act = ya * (1.0 / (1.0 + jnp.exp(-ya))) * yb
        acc = acc + _dot(act.astype(BF16), wdn_ref[ca:ca + FF_CHUNK, :])
    o_ref[0] = _rms(x + acc, gn_ref[...])


def _rope_tables(seq, d):
    inv = ROPE_THETA ** (-jnp.arange(0, d, 2, dtype=F32) / d)
    ang = jnp.arange(seq).astype(F32)[:, None] * inv[None, :]
    cos, sin = jnp.cos(ang), jnp.sin(ang)
    reps = LANES // d
    return (jnp.tile(jnp.concatenate([cos, cos], axis=-1), (1, reps)),
            jnp.tile(jnp.concatenate([-sin, sin], axis=-1), (1, reps)))


def _params(n_axes):
    return pltpu.CompilerParams(dimension_semantics=("arbitrary",) * n_axes,
                                vmem_limit_bytes=VMEM_LIMIT)


def _layer(x, attn_norm, w_in, b_gate, q_norm, w_uq, kv_norm, w_ukv, w_o_mla, w_o_moba,
           w_out, ffn_norm, w_up, conv_w, conv_b, w_down, out_gain):
    B, S, D = x.shape
    H = MLA_HEADS
    assert D == D_MODEL and S % TQ_MLA == 0 and S % TQ_MOBA == 0 and S % MOBA_BLOCK == 0
    assert D_FF % FF_CHUNK == 0 and TQ_MOBA % MOBA_BLOCK == 0
    nb = S // MOBA_BLOCK
    assert nb <= LANES

    wb = w_in.astype(BF16)
    wq = wb[:, OFF_QLAT:OFF_KVLAT]
    wkv = wb[:, OFF_KVLAT:OFF_KPE]
    wpe = jnp.pad(wb[:, OFF_KPE:OFF_MOBA], ((0, 0), (0, LANES - MLA_ROPE)))
    hd = MOBA_HEADS * MOBA_HD
    wmq = wb[:, OFF_MOBA:OFF_MOBA + hd]
    wmk = wb[:, OFF_MOBA + hd:OFF_MOBA + 2 * hd]
    wmv = wb[:, OFF_MOBA + 2 * hd:OFF_GATE]
    wg = wb[:, OFF_GATE:]
    wuq3 = w_uq.astype(BF16).reshape(Q_LORA, H, MLA_NOPE + MLA_ROPE)
    wuqn = wuq3[:, :, :MLA_NOPE].reshape(Q_LORA, H * MLA_NOPE)
    wuqr = wuq3[:, :, MLA_NOPE:].reshape(Q_LORA, H * MLA_ROPE)
    wukv3 = w_ukv.astype(BF16).reshape(KV_LORA, H, MLA_NOPE + MLA_V)
    wuk = wukv3[:, :, :MLA_NOPE].reshape(KV_LORA, H * MLA_NOPE)
    wuv = wukv3[:, :, MLA_NOPE:].reshape(KV_LORA, H * MLA_V)
    cosa, sina = _rope_tables(S, MLA_ROPE)
    cosb, sinb = _rope_tables(S, MOBA_HD)

    row2 = lambda v: v.reshape(1, -1)
    tm = TM_PROJ
    head_out = lambda w, dt: jax.ShapeDtypeStruct((B, H, S, w), dt)
    head_spec = lambda w: pl.BlockSpec((1, H, tm, w), lambda b, s: (b, 0, s, 0))
    tab_spec = pl.BlockSpec((tm, LANES), lambda b, s: (s, 0))
    weights = [row2(attn_norm), wq, wkv, wpe, wmq, wmk, wmv, wg, row2(b_gate), row2(q_norm),
               wuqn, wuqr, row2(kv_norm), wuk, wuv]
    qa, ka, va, qb, kb, vb, g = pl.pallas_call(
        _proj_kernel,
        grid=(B, S // tm),
        in_specs=[pl.BlockSpec((1, tm, D), lambda b, s: (b, s, 0))]
        + [_const_spec(w.shape) for w in weights] + [tab_spec] * 4,
        out_specs=[head_spec(MLA_QK), head_spec(MLA_QK), head_spec(MLA_V),
                   head_spec(MOBA_HD), head_spec(MOBA_HD), head_spec(MOBA_HD),
                   pl.BlockSpec((1, tm, N_BRANCH * D), lambda b, s: (b, s, 0))],
        out_shape=[head_out(MLA_QK, BF16), head_out(MLA_QK, BF16), head_out(MLA_V, BF16),
                   head_out(MOBA_HD, F32), head_out(MOBA_HD, F32), head_out(MOBA_HD, BF16),
                   jax.ShapeDtypeStruct((B, S, N_BRANCH * D), BF16)],
        compiler_params=_params(2),
        name="proj",
    )(x, *weights, cosa, sina, cosb, sinb)

    tq = TQ_MLA
    ya = pl.pallas_call(
        functools.partial(_mla_kernel, tq=tq),
        grid=(B, H, S // tq),
        in_specs=[pl.BlockSpec((1, 1, tq, MLA_QK), lambda b, h, i: (b, h, i, 0)),
                  pl.BlockSpec((1, 1, S, MLA_QK), lambda b, h, i: (b, h, 0, 0)),
                  pl.BlockSpec((1, 1, S, MLA_V), lambda b, h, i: (b, h, 0, 0))],
        out_specs=pl.BlockSpec((1, tq, MLA_V), lambda b, h, i: (b, i, h)),
        out_shape=jax.ShapeDtypeStruct((B, S, H * MLA_V), BF16),
        scratch_shapes=_flash_scratch(tq, MLA_V),
        compiler_params=_params(3),
        name="mla_attn",
    )(qa, ka, va)

    tq = TQ_MOBA
    yb = pl.pallas_call(
        functools.partial(_moba_kernel, tq=tq, nb=nb),
        grid=(B, H, S // tq),
        in_specs=[pl.BlockSpec((1, 1, tq, MOBA_HD), lambda b, h, i: (b, h, i, 0)),
                  pl.BlockSpec((1, 1, S, MOBA_HD), lambda b, h, i: (b, h, 0, 0)),
                  pl.BlockSpec((1, 1, S, MOBA_HD), lambda b, h, i: (b, h, 0, 0))],
        out_specs=pl.BlockSpec((1, tq, MOBA_HD), lambda b, h, i: (b, i, h)),
        out_shape=jax.ShapeDtypeStruct((B, S, H * MOBA_HD), BF16),
        scratch_shapes=[pltpu.VMEM((S, 2 * LANES), BF16),
                        pltpu.VMEM((nb, LANES), F32),
                        pltpu.VMEM((tq, 2 * LANES), BF16)] + _flash_scratch(tq, MOBA_HD),
        compiler_params=_params(3),
        name="moba_attn",
    )(qb, kb, vb)

    tm = TM_MIX
    rows = lambda w: pl.BlockSpec((tm, w), lambda i: (i, 0))
    mix_w = [w_o_mla.astype(BF16), w_o_moba.astype(BF16), w_out.astype(BF16)]
    x1 = pl.pallas_call(
        _mix_kernel,
        grid=(B * S // tm,),
        in_specs=[rows(D), rows(D), rows(D), rows(N_BRANCH * D)]
        + [_const_spec(w.shape) for w in mix_w],
        out_specs=rows(D),
        out_shape=jax.ShapeDtypeStruct((B * S, D), F32),
        compiler_params=_params(1),
        name="mix",
    )(x.reshape(B * S, D), ya.reshape(B * S, D), yb.reshape(B * S, D),
      g.reshape(B * S, N_BRANCH * D), *mix_w)

    tm = TM_FFN
    n_chunks = D_FF // FF_CHUNK
    ffn_w = [row2(ffn_norm), w_up.astype(BF16), conv_w, row2(conv_b), w_down.astype(BF16),
             row2(out_gain)]
    out = pl.pallas_call(
        functools.partial(_ffn_kernel, tm=tm, n_chunks=n_chunks),
        grid=(B, S // tm),
        in_specs=[pl.BlockSpec((1, tm, D), lambda b, s: (b, s, 0))]
        + [_const_spec(w.shape) for w in ffn_w],
        out_specs=pl.BlockSpec((1, tm, D), lambda b, s: (b, s, 0)),
        out_shape=jax.ShapeDtypeStruct((B, S, D), F32),
        scratch_shapes=[pltpu.VMEM((2 * n_chunks, 8, FF_CHUNK), F32)],
        compiler_params=_params(2),
        name="ffn",
    )(x1.reshape(B, S, D), *ffn_w)
    return out


def kernel(x, attn_norm, w_in, b_gate, q_norm, w_uq, kv_norm, w_ukv, w_o_mla, w_o_moba,
           w_out, ffn_norm, w_up, conv_w, conv_b, w_down, final_norm):
    depth = w_in.shape[0]
    assert depth == 1
    return _layer(x, attn_norm[0], w_in[0], b_gate[0], q_norm[0], w_uq[0], kv_norm[0],
                  w_ukv[0], w_o_mla[0], w_o_moba[0], w_out[0], ffn_norm[0], w_up[0],
                  conv_w[0], conv_b[0], w_down[0], final_norm)
```

```python
import functools

import jax
import jax.numpy as jnp
from jax import lax
from jax.experimental import pallas as pl
from jax.experimental.pallas import tpu as pltpu

F32 = jnp.float32
BF16 = jnp.bfloat16

D_MODEL = 1024
MLA_HEADS = 8
MLA_NOPE = 128
MLA_ROPE = 64
MLA_V = 128
Q_LORA = 384
KV_LORA = 256
MOBA_HEADS = 8
MOBA_HD = 128
MOBA_BLOCK = 256
MOBA_TOPK = 3
D_FF = 2816
CONV_W = 3
ROPE_THETA = 10000.0
EPS = 1e-6
NEG = -1e30
N_BRANCH = 2
LOG2E = 1.4426950408889634

OFF_QLAT = 0
OFF_KVLAT = OFF_QLAT + Q_LORA
OFF_KPE = OFF_KVLAT + KV_LORA
OFF_MOBA = OFF_KPE + MLA_ROPE
OFF_GATE = OFF_MOBA + 3 * MOBA_HEADS * MOBA_HD

LANES = 128
SUBLANES = 8
BF16_ROWS = 16
MLA_QK = 2 * LANES
VMEM_LIMIT = 56 * 1024 * 1024

TM_PROJ = 256
ATT_TILE = 1024
QCOLS = 256
GATE_TILE = 512
TM_MIX = 512
TM_FFN = 512
FF_CHUNK = 256


def _dot(a, b):
    return jnp.dot(a, b, preferred_element_type=F32)


def _dot_nt(a, b):
    return lax.dot_general(a, b, (((1,), (1,)), ((), ())), preferred_element_type=F32)


def _rms(x, g):
    return x * lax.rsqrt(jnp.mean(x * x, axis=-1, keepdims=True) + EPS) * g


def _const_spec(shape):
    n = len(shape)
    return pl.BlockSpec(shape, lambda *_: (0,) * n, pipeline_mode=pl.Buffered(1))


def _proj_kernel(x_ref, an_ref, wq_ref, wkv_ref, wpe_ref, wmq_ref, wmk_ref, wmv_ref, wg_ref,
                 bg_ref, qn_ref, wuqn_ref, wuqr_ref, kvn_ref, wuk_ref, wuv_ref,
                 cosa_ref, sina_ref, cosb_ref, sinb_ref,
                 qa_ref, ka_ref, va_ref, qb_ref, kb_ref, vb_ref, g_ref):
    hb = _rms(x_ref[0], an_ref[...]).astype(BF16)
    lane = lax.broadcasted_iota(jnp.int32, (hb.shape[0], LANES), 1)
    low_half = lane < MLA_ROPE
    first32 = (lane % MLA_ROPE) < (MLA_ROPE // 2)
    cosa, sina = cosa_ref[...], sina_ref[...]
    cosb, sinb = cosb_ref[...], sinb_ref[...]

    def rope64(c):
        partner = jnp.where(first32, pltpu.roll(c, LANES - 32, 1), pltpu.roll(c, 32, 1))
        return c * cosa + partner * sina

    def rope128(c):
        return c * cosb + pltpu.roll(c, LANES // 2, 1) * sinb

    ql = _rms(_dot(hb, wq_ref[...]), qn_ref[...]).astype(BF16)
    scale_a = (MLA_NOPE + MLA_ROPE) ** -0.5 * LOG2E
    qn = _dot(ql, wuqn_ref[...]) * scale_a
    qr = _dot(ql, wuqr_ref[...]) * scale_a
    for h in range(MLA_HEADS):
        qa_ref[0, h, :, 0:LANES] = qn[:, h * LANES:(h + 1) * LANES].astype(BF16)
    for c in range(MLA_HEADS // 2):
        r = rope64(qr[:, c * LANES:(c + 1) * LANES])
        qa_ref[0, 2 * c, :, LANES:] = jnp.where(low_half, r, 0.0).astype(BF16)
        qa_ref[0, 2 * c + 1, :, LANES:] = jnp.where(
            low_half, pltpu.roll(r, LANES // 2, 1), 0.0).astype(BF16)

    kvl = _rms(_dot(hb, wkv_ref[...]), kvn_ref[...]).astype(BF16)
    kn = _dot(kvl, wuk_ref[...])
    vt = _dot_nt(wuv_ref[...], kvl)
    krot = rope64(_dot(hb, wpe_ref[...])).astype(BF16)
    for h in range(MLA_HEADS):
        ka_ref[0, h, :, 0:LANES] = kn[:, h * LANES:(h + 1) * LANES].astype(BF16)
        ka_ref[0, h, :, LANES:] = krot
        va_ref[0, h, 0] = vt[h * LANES:(h + 1) * LANES, :].astype(BF16)

    mq = _dot(hb, wmq_ref[...])
    for h in range(MOBA_HEADS):
        qb_ref[0, h] = rope128(mq[:, h * LANES:(h + 1) * LANES])
    mk = _dot(hb, wmk_ref[...])
    for h in range(MOBA_HEADS):
        kb_ref[0, h] = rope128(mk[:, h * LANES:(h + 1) * LANES])
    mvt = _dot_nt(wmv_ref[...], hb)
    for h in range(MOBA_HEADS):
        vb_ref[0, h, 0] = mvt[h * LANES:(h + 1) * LANES, :].astype(BF16)

    z = _dot(hb, wg_ref[...]) + bg_ref[...]
    g_ref[0] = (1.0 / (1.0 + jnp.exp(-z))).astype(BF16)


def _flash_causal(nq, T, q_rows, k_tile, vt_tile, o_ref, scr):
    s_refs, x_refs, p_refs, a_refs = scr[0:2], scr[2:4], scr[4:6], scr[6:8]
    m_all, l_all, acc_all = scr[8:11]

    groups = range(T // QCOLS)
    cols = lambda c: slice(c * QCOLS, (c + 1) * QCOLS)

    def scores(i, kt, slot, masked, c):
        s = _dot_nt(k_tile(kt), q_rows(i * T + c * QCOLS))
        if masked:
            key = lax.broadcasted_iota(jnp.int32, (T, QCOLS), 0)
            qry = lax.broadcasted_iota(jnp.int32, (T, QCOLS), 1) + c * QCOLS
            s = jnp.where(key <= qry, s, NEG)
        s_refs[slot][:, cols(c)] = s
        x_refs[slot][:, cols(c)] = jnp.broadcast_to(jnp.max(s, axis=0, keepdims=True),
                                                    (SUBLANES, QCOLS))

    def softmax(i, slot, first, c):
        if first:
            m_new = x_refs[slot][:, cols(c)]
        else:
            m_old = m_all[i, :, cols(c)]
            m_new = jnp.maximum(m_old, x_refs[slot][:, cols(c)])
            alpha = jnp.exp2(m_old - m_new)
            a_refs[slot][:, cols(c)] = alpha
        psum = None
        for r in range(T // BF16_ROWS):
            rows = slice(r * BF16_ROWS, (r + 1) * BF16_ROWS)
            p = jnp.exp2(s_refs[slot][rows, cols(c)] - m_new[0:1, :])
            p_refs[slot][rows, cols(c)] = p.astype(BF16)
            part = p[:SUBLANES] + p[SUBLANES:]
            psum = part if psum is None else psum + part
        m_all[i, :, cols(c)] = m_new
        l_all[i, :, cols(c)] = psum if first else alpha * l_all[i, :, cols(c)] + psum

    def pv(i, kt, slot, first, c):
        r = _dot(vt_tile(kt), p_refs[slot][:, cols(c)])
        acc_all[i, :, cols(c)] = (r if first else
                                  a_refs[slot][0:1, cols(c)] * acc_all[i, :, cols(c)] + r)

    def by_parity(n, fn):
        @pl.when(n % 2 == 0)
        def _():
            fn(0, 1)

        @pl.when(n % 2 == 1)
        def _():
            fn(1, 0)

    for c in groups:
        scores(0, 0, 0, True, c)
    for c in groups:
        scores(1, 1, 1, True, c)
        softmax(0, 0, True, c)

    def diag_step(n, carry):
        def fn(cur, prv):
            nxt = jnp.minimum(n + 1, nq - 1)
            for c in groups:
                scores(nxt, nxt, prv, True, c)
                softmax(n, cur, True, c)
                pv(n - 1, n - 1, prv, True, c)
        by_parity(n, fn)
        return carry

    lax.fori_loop(1, nq, diag_step, 0)
    for c in groups:
        pv(nq - 1, nq - 1, (nq - 1) % 2, True, c)

    n_items = nq * (nq - 1) // 2

    def advance(i, kt):
        wrap = kt + 1 >= i
        return jnp.where(wrap, i + 1, i), jnp.where(wrap, 0, kt + 1)

    for c in groups:
        scores(1, 0, 0, False, c)
    for c in groups:
        scores(2, 0, 1, False, c)
        softmax(1, 0, False, c)

    def full_step(n, carry):
        ip, kp, ic, kc = carry
        i_n, k_n = advance(ic, kc)
        i_n = jnp.minimum(i_n, nq - 1)
        k_n = jnp.minimum(k_n, nq - 2)

        def fn(cur, prv):
            for c in groups:
                scores(i_n, k_n, prv, False, c)
                softmax(ic, cur, False, c)
                pv(ip, kp, prv, False, c)
        by_parity(n, fn)
        return ic, kc, i_n, k_n

    ip, kp, _, _ = lax.fori_loop(1, n_items, full_step,
                                 (jnp.int32(1), jnp.int32(0), jnp.int32(2), jnp.int32(0)))
    for c in groups:
        pv(ip, kp, (n_items - 1) % 2, False, c)

    def finish(i, c):
        l = jnp.sum(l_all[i], axis=0, keepdims=True)
        o_ref[0, pl.ds(pl.multiple_of(i * T, T), T), :] = (acc_all[i] / l).T.astype(BF16)
        return c

    lax.fori_loop(0, nq, finish, 0)


def _flash_scratch(nq, T, dv):
    return ([pltpu.VMEM((T, T), F32)] * 2 + [pltpu.VMEM((SUBLANES, T), F32)] * 2
            + [pltpu.VMEM((T, T), BF16)] * 2 + [pltpu.VMEM((SUBLANES, T), F32)] * 2
            + [pltpu.VMEM((nq, SUBLANES, T), F32)] * 2 + [pltpu.VMEM((nq, dv, T), F32)])


def _q_rows(ref):
    return lambda start: ref[pl.ds(pl.multiple_of(start, QCOLS), QCOLS), :]


def _k_tile(ref, T):
    return lambda kt: ref[pl.ds(pl.multiple_of(kt * T, T), T), :]


def _mla_kernel(q_ref, k_ref, vt_ref, o_ref, *scr, nq, T):
    _flash_causal(nq, T, _q_rows(q_ref.at[0, 0]), _k_tile(k_ref.at[0, 0], T),
                  lambda kt: vt_ref[0, 0, kt], o_ref, scr)


def _moba_kernel(q_ref, k_ref, vt_ref, o_ref, kaug_ref, qaug_ref, kmean_ref, *scr, nq, T, nb):
    lane = lax.broadcasted_iota(jnp.int32, (MOBA_BLOCK, LANES), 1)

    def blk_body(b, c):
        off = pl.multiple_of(b * MOBA_BLOCK, MOBA_BLOCK)
        kb = k_ref[0, 0, pl.ds(off, MOBA_BLOCK), :]
        kmean_ref[pl.ds(b, 1), :] = jnp.mean(kb, axis=0, keepdims=True)
        kaug_ref[pl.ds(off, MOBA_BLOCK), 0:LANES] = kb.astype(BF16)
        kaug_ref[pl.ds(off, MOBA_BLOCK), LANES:] = jnp.where(lane == b, 1.0, 0.0).astype(BF16)
        return c

    lax.fori_loop(0, nb, blk_body, 0)

    gt_n = GATE_TILE
    kblk = lax.broadcasted_iota(jnp.int32, (nb, gt_n), 0)
    qoff = lax.broadcasted_iota(jnp.int32, (nb, gt_n), 1) // MOBA_BLOCK

    def gate_body(t, c):
        off = pl.multiple_of(t * gt_n, gt_n)
        q = q_ref[0, 0, pl.ds(off, gt_n), :]
        gt = lax.dot_general(kmean_ref[...], q, (((1,), (1,)), ((), ())),
                             precision=lax.Precision.HIGHEST, preferred_element_type=F32)
        qblk = t * (gt_n // MOBA_BLOCK) + qoff
        past = kblk < qblk
        g = jnp.where(past, gt, -jnp.inf)
        cnt = jnp.zeros((nb, gt_n), F32)
        for jp in range(nb):
            gj = g[jp:jp + 1, :]
            beats = (gj > g) | ((gj == g) & (jp < kblk))
            cnt = cnt + jnp.where(beats, 1.0, 0.0)
        keep = (past & (cnt < float(MOBA_TOPK))) | (kblk == qblk)
        bias_t = jnp.where(keep, 0.0, NEG)
        bias_t = jnp.concatenate([bias_t, jnp.zeros((LANES - nb, gt_n), F32)], axis=0)
        qaug_ref[pl.ds(off, gt_n), 0:LANES] = (q * (MOBA_HD ** -0.5 * LOG2E)).astype(BF16)
        qaug_ref[pl.ds(off, gt_n), LANES:] = bias_t.T.astype(BF16)
        return c

    lax.fori_loop(0, nq * T // gt_n, gate_body, 0)

    _flash_causal(nq, T, _q_rows(qaug_ref), _k_tile(kaug_ref, T),
                  lambda kt: vt_ref[0, 0, kt], o_ref, scr)


def _mix_kernel(x_ref, ya_ref, yb_ref, g_ref, woa_ref, wob_ref, wout_ref, o_ref):
    a = _dot(ya_ref[...], woa_ref[...])
    b = _dot(yb_ref[...], wob_ref[...])
    g = g_ref[...].astype(F32)
    mix = g[:, :D_MODEL] * a + g[:, D_MODEL:] * b
    o_ref[...] = x_ref[...] + _dot(mix.astype(BF16), wout_ref[...])


def _ffn_kernel(x_ref, fn_ref, wup_ref, cw_ref, cb_ref, wdn_ref, gn_ref, o_ref, carry_ref,
                *, tm, n_chunks):
    si = pl.program_id(1)

    @pl.when(si == 0)
    def _():
        carry_ref[...] = jnp.zeros_like(carry_ref)

    x = x_ref[0]
    hb = _rms(x, fn_ref[...]).astype(BF16)
    row = lax.broadcasted_iota(jnp.int32, (tm, FF_CHUNK), 0)

    def conv(u, idx, col0):
        prev = carry_ref[idx]
        u1 = jnp.where(row == 0, prev[7:8, :], pltpu.roll(u, 1, 0))
        u2 = jnp.where(row == 0, prev[6:7, :],
                       jnp.where(row == 1, prev[7:8, :], pltpu.roll(u, 2, 0)))
        carry_ref[idx] = u[tm - 8:, :]
        w = cw_ref[:, col0:col0 + FF_CHUNK]
        return cb_ref[:, col0:col0 + FF_CHUNK] + w[0:1] * u2 + w[1:2] * u1 + w[2:3] * u

    acc = jnp.zeros((tm, D_MODEL), F32)
    for c in range(n_chunks):
        ca = c * FF_CHUNK
        cb = D_FF + c * FF_CHUNK
        ya = conv(_dot(hb, wup_ref[:, ca:ca + FF_CHUNK]), c, ca)
        yb = conv(_dot(hb, wup_ref[:, cb:cb + FF_CHUNK]), n_chunks + c, cb)
        act = ya * (1.0 / (1.0 + jnp.exp(-ya))) * yb
        acc = acc + _dot(act.astype(BF16), wdn_ref[ca:ca + FF_CHUNK, :])
    o_ref[0] = _rms(x + acc, gn_ref[...])


def _rope_tables(seq, d):
    inv = ROPE_THETA ** (-jnp.arange(0, d, 2, dtype=F32) / d)
    ang = jnp.arange(seq).astype(F32)[:, None] * inv[None, :]
    cos, sin = jnp.cos(ang), jnp.sin(ang)
    reps = LANES // d
    return (jnp.tile(jnp.concatenate([cos, cos], axis=-1), (1, reps)),
            jnp.tile(jnp.concatenate([-sin, sin], axis=-1), (1, reps)))


def _params(n_axes):
    return pltpu.CompilerParams(dimension_semantics=("arbitrary",) * n_axes,
                                vmem_limit_bytes=VMEM_LIMIT)


def _layer(x, attn_norm, w_in, b_gate, q_norm, w_uq, kv_norm, w_ukv, w_o_mla, w_o_moba,
           w_out, ffn_norm, w_up, conv_w, conv_b, w_down, out_gain):
    B, S, D = x.shape
    H = MLA_HEADS
    T = ATT_TILE
    assert D == D_MODEL and S % T == 0 and S // T >= 3 and T % GATE_TILE == 0
    assert D_FF % FF_CHUNK == 0 and GATE_TILE % MOBA_BLOCK == 0 and T % TM_PROJ == 0
    nq = S // T
    nb = S // MOBA_BLOCK
    assert nb <= LANES and nb % SUBLANES == 0

    wb = w_in.astype(BF16)
    wq = wb[:, OFF_QLAT:OFF_KVLAT]
    wkv = wb[:, OFF_KVLAT:OFF_KPE]
    wpe = jnp.pad(wb[:, OFF_KPE:OFF_MOBA], ((0, 0), (0, LANES - MLA_ROPE)))
    hd = MOBA_HEADS * MOBA_HD
    wmq = wb[:, OFF_MOBA:OFF_MOBA + hd]
    wmk = wb[:, OFF_MOBA + hd:OFF_MOBA + 2 * hd]
    wmv = wb[:, OFF_MOBA + 2 * hd:OFF_GATE].T
    wg = wb[:, OFF_GATE:]
    wuq3 = w_uq.astype(BF16).reshape(Q_LORA, H, MLA_NOPE + MLA_ROPE)
    wuqn = wuq3[:, :, :MLA_NOPE].reshape(Q_LORA, H * MLA_NOPE)
    wuqr = wuq3[:, :, MLA_NOPE:].reshape(Q_LORA, H * MLA_ROPE)
    wukv3 = w_ukv.astype(BF16).reshape(KV_LORA, H, MLA_NOPE + MLA_V)
    wuk = wukv3[:, :, :MLA_NOPE].reshape(KV_LORA, H * MLA_NOPE)
    wuv = wukv3[:, :, MLA_NOPE:].reshape(KV_LORA, H * MLA_V).T
    cosa, sina = _rope_tables(S, MLA_ROPE)
    cosb, sinb = _rope_tables(S, MOBA_HD)

    row2 = lambda v: v.reshape(1, -1)
    tm = TM_PROJ
    head_out = lambda w, dt: jax.ShapeDtypeStruct((B, H, S, w), dt)
    head_spec = lambda w: pl.BlockSpec((1, H, tm, w), lambda b, s: (b, 0, s, 0))
    tab_spec = pl.BlockSpec((tm, LANES), lambda b, s: (s, 0))
    sub = T // tm
    vt_out = lambda w: jax.ShapeDtypeStruct((B, H, nq, w, T), BF16)
    vt_spec = lambda w: pl.BlockSpec((1, H, 1, w, tm), lambda b, s: (b, 0, s // sub, 0, s % sub))
    weights = [row2(attn_norm), wq, wkv, wpe, wmq, wmk, wmv, wg, row2(b_gate), row2(q_norm),
               wuqn, wuqr, row2(kv_norm), wuk, wuv]
    qa, ka, va, qb, kb, vb, g = pl.pallas_call(
        _proj_kernel,
        grid=(B, S // tm),
        in_specs=[pl.BlockSpec((1, tm, D), lambda b, s: (b, s, 0))]
        + [_const_spec(w.shape) for w in weights] + [tab_spec] * 4,
        out_specs=[head_spec(MLA_QK), head_spec(MLA_QK), vt_spec(MLA_V),
                   head_spec(MOBA_HD), head_spec(MOBA_HD), vt_spec(MOBA_HD),
                   pl.BlockSpec((1, tm, N_BRANCH * D), lambda b, s: (b, s, 0))],
        out_shape=[head_out(MLA_QK, BF16), head_out(MLA_QK, BF16), vt_out(MLA_V),
                   head_out(MOBA_HD, F32), head_out(MOBA_HD, F32), vt_out(MOBA_HD),
                   jax.ShapeDtypeStruct((B, S, N_BRANCH * D), BF16)],
        compiler_params=_params(2),
        name="proj",
    )(x, *weights, cosa, sina, cosb, sinb)

    seq_spec = lambda w: pl.BlockSpec((1, 1, S, w), lambda b, h: (b, h, 0, 0))
    vt_in = lambda w: pl.BlockSpec((1, 1, nq, w, T), lambda b, h: (b, h, 0, 0, 0))
    att_out = pl.BlockSpec((1, S, LANES), lambda b, h: (b, 0, h))
    ya = pl.pallas_call(
        functools.partial(_mla_kernel, nq=nq, T=T),
        grid=(B, H),
        in_specs=[seq_spec(MLA_QK), seq_spec(MLA_QK), vt_in(MLA_V)],
        out_specs=att_out,
        out_shape=jax.ShapeDtypeStruct((B, S, H * MLA_V), BF16),
        scratch_shapes=_flash_scratch(nq, T, MLA_V),
        compiler_params=_params(2),
        name="mla_attn",
    )(qa, ka, va)

    yb = pl.pallas_call(
        functools.partial(_moba_kernel, nq=nq, T=T, nb=nb),
        grid=(B, H),
        in_specs=[seq_spec(MOBA_HD), seq_spec(MOBA_HD), vt_in(MOBA_HD)],
        out_specs=att_out,
        out_shape=jax.ShapeDtypeStruct((B, S, H * MOBA_HD), BF16),
        scratch_shapes=[pltpu.VMEM((S, 2 * LANES), BF16), pltpu.VMEM((S, 2 * LANES), BF16),
                        pltpu.VMEM((nb, LANES), F32)] + _flash_scratch(nq, T, MOBA_HD),
        compiler_params=_params(2),
        name="moba_attn",
    )(qb, kb, vb)

    tm = TM_MIX
    rows = lambda w: pl.BlockSpec((tm, w), lambda i: (i, 0))
    mix_w = [w_o_mla.astype(BF16), w_o_moba.astype(BF16), w_out.astype(BF16)]
    x1 = pl.pallas_call(
        _mix_kernel,
        grid=(B * S // tm,),
        in_specs=[rows(D), rows(D), rows(D), rows(N_BRANCH * D)]
        + [_const_spec(w.shape) for w in mix_w],
        out_specs=rows(D),
        out_shape=jax.ShapeDtypeStruct((B * S, D), F32),
        compiler_params=_params(1),
        name="mix",
    )(x.reshape(B * S, D), ya.reshape(B * S, D), yb.reshape(B * S, D),
      g.reshape(B * S, N_BRANCH * D), *mix_w)

    tm = TM_FFN
    n_chunks = D_FF // FF_CHUNK
    ffn_w = [row2(ffn_norm), w_up.astype(BF16), conv_w, row2(conv_b), w_down.astype(BF16),
             row2(out_gain)]
    out = pl.pallas_call(
        functools.partial(_ffn_kernel, tm=tm, n_chunks=n_chunks),
        grid=(B, S // tm),
        in_specs=[pl.BlockSpec((1, tm, D), lambda b, s: (b, s, 0))]
        + [_const_spec(w.shape) for w in ffn_w],
        out_specs=pl.BlockSpec((1, tm, D), lambda b, s: (b, s, 0)),
        out_shape=jax.ShapeDtypeStruct((B, S, D), F32),
        scratch_shapes=[pltpu.VMEM((2 * n_chunks, 8, FF_CHUNK), F32)],
        compiler_params=_params(2),
        name="ffn",
    )(x1.reshape(B, S, D), *ffn_w)
    return out


def kernel(x, attn_norm, w_in, b_gate, q_norm, w_uq, kv_norm, w_ukv, w_o_mla, w_o_moba,
           w_out, ffn_norm, w_up, conv_w, conv_b, w_down, final_norm):
    depth = w_in.shape[0]
    assert depth == 1
    return _layer(x, attn_norm[0], w_in[0], b_gate[0], q_norm[0], w_uq[0], kv_norm[0],
                  w_ukv[0], w_o_mla[0], w_o_moba[0], w_out[0], ffn_norm[0], w_up[0],
                  conv_w[0], conv_b[0], w_down[0], final_norm)
```

```python
import functools

import jax
import jax.numpy as jnp
from jax import lax
from jax.experimental import pallas as pl
from jax.experimental.pallas import tpu as pltpu

F32 = jnp.float32
BF16 = jnp.bfloat16

D_MODEL = 1024
MLA_HEADS = 8
MLA_NOPE = 128
MLA_ROPE = 64
MLA_V = 128
Q_LORA = 384
KV_LORA = 256
MOBA_HEADS = 8
MOBA_HD = 128
MOBA_BLOCK = 256
MOBA_TOPK = 3
D_FF = 2816
CONV_W = 3
ROPE_THETA = 10000.0
EPS = 1e-6
NEG = -1e30
N_BRANCH = 2
LOG2E = 1.4426950408889634

OFF_QLAT = 0
OFF_KVLAT = OFF_QLAT + Q_LORA
OFF_KPE = OFF_KVLAT + KV_LORA
OFF_MOBA = OFF_KPE + MLA_ROPE
OFF_GATE = OFF_MOBA + 3 * MOBA_HEADS * MOBA_HD

LANES = 128
SUBLANES = 8
BF16_ROWS = 16
MLA_QK = 2 * LANES
VMEM_LIMIT = 56 * 1024 * 1024

TM_PROJ = 256
ATT_TILE = 1024
QCOLS = 256
GATE_TILE = 512
TM_MIX = 512
TM_FFN = 512
FF_CHUNK = 256


def _dot(a, b):
    return jnp.dot(a, b, preferred_element_type=F32)


def _dot_nt(a, b):
    return lax.dot_general(a, b, (((1,), (1,)), ((), ())), preferred_element_type=F32)


def _rms(x, g):
    return x * lax.rsqrt(jnp.mean(x * x, axis=-1, keepdims=True) + EPS) * g


def _const_spec(shape):
    n = len(shape)
    return pl.BlockSpec(shape, lambda *_: (0,) * n, pipeline_mode=pl.Buffered(1))


def _proj_kernel(x_ref, an_ref, wq_ref, wkv_ref, wpe_ref, wmq_ref, wmk_ref, wmv_ref, wg_ref,
                 bg_ref, qn_ref, wuqn_ref, wuqr_ref, kvn_ref, wuk_ref, wuv_ref,
                 cosa_ref, sina_ref, cosb_ref, sinb_ref,
                 qa_ref, ka_ref, va_ref, qb_ref, kb_ref, vb_ref, g_ref):
    hb = _rms(x_ref[0], an_ref[...]).astype(BF16)
    lane = lax.broadcasted_iota(jnp.int32, (hb.shape[0], LANES), 1)
    low_half = lane < MLA_ROPE
    first32 = (lane % MLA_ROPE) < (MLA_ROPE // 2)
    cosa, sina = cosa_ref[...], sina_ref[...]
    cosb, sinb = cosb_ref[...], sinb_ref[...]

    def rope64(c):
        partner = jnp.where(first32, pltpu.roll(c, LANES - 32, 1), pltpu.roll(c, 32, 1))
        return c * cosa + partner * sina

    def rope128(c):
        return c * cosb + pltpu.roll(c, LANES // 2, 1) * sinb

    ql = _rms(_dot(hb, wq_ref[...]), qn_ref[...]).astype(BF16)
    scale_a = (MLA_NOPE + MLA_ROPE) ** -0.5 * LOG2E
    qn = _dot(ql, wuqn_ref[...]) * scale_a
    qr = _dot(ql, wuqr_ref[...]) * scale_a
    for h in range(MLA_HEADS):
        qa_ref[0, h, :, 0:LANES] = qn[:, h * LANES:(h + 1) * LANES].astype(BF16)
    for c in range(MLA_HEADS // 2):
        r = rope64(qr[:, c * LANES:(c + 1) * LANES])
        qa_ref[0, 2 * c, :, LANES:] = jnp.where(low_half, r, 0.0).astype(BF16)
        qa_ref[0, 2 * c + 1, :, LANES:] = jnp.where(
            low_half, pltpu.roll(r, LANES // 2, 1), 0.0).astype(BF16)

    kvl = _rms(_dot(hb, wkv_ref[...]), kvn_ref[...]).astype(BF16)
    kn = _dot(kvl, wuk_ref[...])
    vt = _dot_nt(wuv_ref[...], kvl)
    krot = rope64(_dot(hb, wpe_ref[...])).astype(BF16)
    for h in range(MLA_HEADS):
        ka_ref[0, h, :, 0:LANES] = kn[:, h * LANES:(h + 1) * LANES].astype(BF16)
        ka_ref[0, h, :, LANES:] = krot
        va_ref[0, h, 0] = vt[h * LANES:(h + 1) * LANES, :].astype(BF16)

    mq = _dot(hb, wmq_ref[...])
    for h in range(MOBA_HEADS):
        qb_ref[0, h] = rope128(mq[:, h * LANES:(h + 1) * LANES])
    mk = _dot(hb, wmk_ref[...])
    for h in range(MOBA_HEADS):
        kb_ref[0, h] = rope128(mk[:, h * LANES:(h + 1) * LANES])
    mvt = _dot_nt(wmv_ref[...], hb)
    for h in range(MOBA_HEADS):
        vb_ref[0, h, 0] = mvt[h * LANES:(h + 1) * LANES, :].astype(BF16)

    z = _dot(hb, wg_ref[...]) + bg_ref[...]
    g_ref[0] = (1.0 / (1.0 + jnp.exp(-z))).astype(BF16)


def _flash_causal(nq, T, q_rows, k_tile, vt_tile, o_ref, scr):
    s_refs, x_refs, p_refs, a_refs = scr[0:2], scr[2:4], scr[4:6], scr[6:8]
    m_all, l_all, acc_all = scr[8:11]

    groups = range(T // QCOLS)
    cols = lambda c: slice(c * QCOLS, (c + 1) * QCOLS)

    def scores(i, kt, slot, masked, c):
        s = _dot_nt(k_tile(kt), q_rows(i * T + c * QCOLS))
        if masked:
            key = lax.broadcasted_iota(jnp.int32, (T, QCOLS), 0)
            qry = lax.broadcasted_iota(jnp.int32, (T, QCOLS), 1) + c * QCOLS
            s = jnp.where(key <= qry, s, NEG)
        s_refs[slot][:, cols(c)] = s
        x_refs[slot][:, cols(c)] = jnp.broadcast_to(jnp.max(s, axis=0, keepdims=True),
                                                    (SUBLANES, QCOLS))

    def softmax(i, slot, first, c):
        if first:
            m_new = x_refs[slot][:, cols(c)]
        else:
            m_old = m_all[i, :, cols(c)]
            m_new = jnp.maximum(m_old, x_refs[slot][:, cols(c)])
            alpha = jnp.exp2(m_old - m_new)
            a_refs[slot][:, cols(c)] = alpha
        psum = None
        for r in range(T // BF16_ROWS):
            rows = slice(r * BF16_ROWS, (r + 1) * BF16_ROWS)
            p = jnp.exp2(s_refs[slot][rows, cols(c)] - m_new[0:1, :])
            p_refs[slot][rows, cols(c)] = p.astype(BF16)
            part = p[:SUBLANES] + p[SUBLANES:]
            psum = part if psum is None else psum + part
        m_all[i, :, cols(c)] = m_new
        l_all[i, :, cols(c)] = psum if first else alpha * l_all[i, :, cols(c)] + psum

    def pv(i, kt, slot, first, c):
        r = _dot(vt_tile(kt), p_refs[slot][:, cols(c)])
        acc_all[i, :, cols(c)] = (r if first else
                                  a_refs[slot][0:1, cols(c)] * acc_all[i, :, cols(c)] + r)

    def by_parity(n, fn):
        @pl.when(n % 2 == 0)
        def _():
            fn(0, 1)

        @pl.when(n % 2 == 1)
        def _():
            fn(1, 0)

    for c in groups:
        scores(0, 0, 0, True, c)
    for c in groups:
        scores(1, 1, 1, True, c)
        softmax(0, 0, True, c)

    def diag_step(n, carry):
        def fn(cur, prv):
            nxt = jnp.minimum(n + 1, nq - 1)
            for c in groups:
                scores(nxt, nxt, prv, True, c)
                softmax(n, cur, True, c)
                pv(n - 1, n - 1, prv, True, c)
        by_parity(n, fn)
        return carry

    lax.fori_loop(1, nq, diag_step, 0)
    for c in groups:
        pv(nq - 1, nq - 1, (nq - 1) % 2, True, c)

    n_items = nq * (nq - 1) // 2

    def advance(i, kt):
        wrap = kt + 1 >= i
        return jnp.where(wrap, i + 1, i), jnp.where(wrap, 0, kt + 1)

    for c in groups:
        scores(1, 0, 0, False, c)
    for c in groups:
        scores(2, 0, 1, False, c)
        softmax(1, 0, False, c)

    def full_step(n, carry):
        ip, kp, ic, kc = carry
        i_n, k_n = advance(ic, kc)
        i_n = jnp.minimum(i_n, nq - 1)
        k_n = jnp.minimum(k_n, nq - 2)

        def fn(cur, prv):
            for c in groups:
                scores(i_n, k_n, prv, False, c)
                softmax(ic, cur, False, c)
                pv(ip, kp, prv, False, c)
        by_parity(n, fn)
        return ic, kc, i_n, k_n

    ip, kp, _, _ = lax.fori_loop(1, n_items, full_step,
                                 (jnp.int32(1), jnp.int32(0), jnp.int32(2), jnp.int32(0)))
    for c in groups:
        pv(ip, kp, (n_items - 1) % 2, False, c)

    def finish(i, c):
        l = jnp.sum(l_all[i], axis=0, keepdims=True)
        o_ref[0, pl.ds(pl.multiple_of(i * T, T), T), :] = (acc_all[i] / l).T.astype(BF16)
        return c

    lax.fori_loop(0, nq, finish, 0)


def _flash_scratch(nq, T, dv):
    return ([pltpu.VMEM((T, T), F32)] * 2 + [pltpu.VMEM((SUBLANES, T), F32)] * 2
            + [pltpu.VMEM((T, T), BF16)] * 2 + [pltpu.VMEM((SUBLANES, T), F32)] * 2
            + [pltpu.VMEM((nq, SUBLANES, T), F32)] * 2 + [pltpu.VMEM((nq, dv, T), F32)])


def _q_rows(ref):
    return lambda start: ref[pl.ds(pl.multiple_of(start, QCOLS), QCOLS), :]


def _k_tile(ref, T):
    return lambda kt: ref[pl.ds(pl.multiple_of(kt * T, T), T), :]


def _mla_kernel(q_ref, k_ref, vt_ref, o_ref, *scr, nq, T):
    _flash_causal(nq, T, _q_rows(q_ref.at[0, 0]), _k_tile(k_ref.at[0, 0], T),
                  lambda kt: vt_ref[0, 0, kt], o_ref, scr)


def _moba_kernel(q_ref, k_ref, vt_ref, o_ref, kaug_ref, qaug_ref, kmean_ref, *scr, nq, T, nb):
    lane = lax.broadcasted_iota(jnp.int32, (MOBA_BLOCK, LANES), 1)

    def blk_body(b, c):
        off = pl.multiple_of(b * MOBA_BLOCK, MOBA_BLOCK)
        kb = k_ref[0, 0, pl.ds(off, MOBA_BLOCK), :]
        kmean_ref[pl.ds(b, 1), :] = jnp.mean(kb, axis=0, keepdims=True)
        kaug_ref[pl.ds(off, MOBA_BLOCK), 0:LANES] = kb.astype(BF16)
        kaug_ref[pl.ds(off, MOBA_BLOCK), LANES:] = jnp.where(lane == b, 1.0, 0.0).astype(BF16)
        return c

    lax.fori_loop(0, nb, blk_body, 0)

    gt_n = GATE_TILE
    kblk = lax.broadcasted_iota(jnp.int32, (nb, gt_n), 0)
    kblk_f = kblk.astype(F32)
    qoff = lax.broadcasted_iota(jnp.int32, (nb, gt_n), 1) // MOBA_BLOCK

    def gate_body(t, c):
        off = pl.multiple_of(t * gt_n, gt_n)
        q = q_ref[0, 0, pl.ds(off, gt_n), :]
        gt = lax.dot_general(kmean_ref[...], q, (((1,), (1,)), ((), ())),
                             precision=lax.Precision.HIGHEST, preferred_element_type=F32)
        qblk = t * (gt_n // MOBA_BLOCK) + qoff
        past = kblk < qblk
        g = jnp.where(past, gt, -jnp.inf)
        sel = kblk == qblk
        for _ in range(MOBA_TOPK):
            first = jnp.min(jnp.where(g == jnp.max(g, axis=0, keepdims=True), kblk_f, float(nb)),
                            axis=0, keepdims=True)
            hit = kblk_f == first
            sel = sel | (hit & past)
            g = jnp.where(hit, -jnp.inf, g)
        keep = sel
        bias_t = jnp.where(keep, 0.0, NEG)
        bias_t = jnp.concatenate([bias_t, jnp.zeros((LANES - nb, gt_n), F32)], axis=0)
        qaug_ref[pl.ds(off, gt_n), 0:LANES] = (q * (MOBA_HD ** -0.5 * LOG2E)).astype(BF16)
        qaug_ref[pl.ds(off, gt_n), LANES:] = bias_t.T.astype(BF16)
        return c

    lax.fori_loop(0, nq * T // gt_n, gate_body, 0)

    _flash_causal(nq, T, _q_rows(qaug_ref), _k_tile(kaug_ref, T),
                  lambda kt: vt_ref[0, 0, kt], o_ref, scr)


def _mix_kernel(x_ref, ya_ref, yb_ref, g_ref, woa_ref, wob_ref, wout_ref, o_ref):
    a = _dot(ya_ref[...], woa_ref[...])
    b = _dot(yb_ref[...], wob_ref[...])
    g = g_ref[...].astype(F32)
    mix = g[:, :D_MODEL] * a + g[:, D_MODEL:] * b
    o_ref[...] = x_ref[...] + _dot(mix.astype(BF16), wout_ref[...])


def _ffn_kernel(x_ref, fn_ref, wup_ref, cw_ref, cb_ref, wdn_ref, gn_ref, o_ref, carry_ref,
                act_ref, *bufs, tm, n_chunks):
    si = pl.program_id(1)

    @pl.when(si == 0)
    def _():
        carry_ref[...] = jnp.zeros_like(carry_ref)

    x = x_ref[0]
    hb = _rms(x, fn_ref[...]).astype(BF16)

    def conv(u, buf, idx, col0):
        buf[0:SUBLANES, :] = carry_ref[idx]
        buf[SUBLANES:, :] = u
        carry_ref[idx] = u[tm - SUBLANES:, :]
        u1 = buf[SUBLANES - 1:SUBLANES - 1 + tm, :]
        u2 = buf[SUBLANES - 2:SUBLANES - 2 + tm, :]
        w = cw_ref[:, col0:col0 + FF_CHUNK]
        return cb_ref[:, col0:col0 + FF_CHUNK] + w[0:1] * u2 + w[1:2] * u1 + w[2:3] * u

    for c in range(n_chunks):
        ca = c * FF_CHUNK
        cb = D_FF + c * FF_CHUNK
        ya = conv(_dot(hb, wup_ref[:, ca:ca + FF_CHUNK]), bufs[2 * (c % 2)], c, ca)
        yb = conv(_dot(hb, wup_ref[:, cb:cb + FF_CHUNK]), bufs[2 * (c % 2) + 1], n_chunks + c, cb)
        act_ref[:, ca:ca + FF_CHUNK] = (ya * (1.0 / (1.0 + jnp.exp(-ya))) * yb).astype(BF16)
    o_ref[0] = _rms(x + _dot(act_ref[...], wdn_ref[...]), gn_ref[...])


def _rope_tables(seq, d):
    inv = ROPE_THETA ** (-jnp.arange(0, d, 2, dtype=F32) / d)
    ang = jnp.arange(seq).astype(F32)[:, None] * inv[None, :]
    cos, sin = jnp.cos(ang), jnp.sin(ang)
    reps = LANES // d
    return (jnp.tile(jnp.concatenate([cos, cos], axis=-1), (1, reps)),
            jnp.tile(jnp.concatenate([-sin, sin], axis=-1), (1, reps)))


def _params(n_axes):
    return pltpu.CompilerParams(dimension_semantics=("arbitrary",) * n_axes,
                                vmem_limit_bytes=VMEM_LIMIT)


def _layer(x, attn_norm, w_in, b_gate, q_norm, w_uq, kv_norm, w_ukv, w_o_mla, w_o_moba,
           w_out, ffn_norm, w_up, conv_w, conv_b, w_down, out_gain):
    B, S, D = x.shape
    H = MLA_HEADS
    T = ATT_TILE
    assert D == D_MODEL and S % T == 0 and S // T >= 3 and T % GATE_TILE == 0
    assert D_FF % FF_CHUNK == 0 and GATE_TILE % MOBA_BLOCK == 0 and T % TM_PROJ == 0
    nq = S // T
    nb = S // MOBA_BLOCK
    assert nb <= LANES and nb % SUBLANES == 0

    wb = w_in.astype(BF16)
    wq = wb[:, OFF_QLAT:OFF_KVLAT]
    wkv = wb[:, OFF_KVLAT:OFF_KPE]
    wpe = jnp.pad(wb[:, OFF_KPE:OFF_MOBA], ((0, 0), (0, LANES - MLA_ROPE)))
    hd = MOBA_HEADS * MOBA_HD
    wmq = wb[:, OFF_MOBA:OFF_MOBA + hd]
    wmk = wb[:, OFF_MOBA + hd:OFF_MOBA + 2 * hd]
    wmv = wb[:, OFF_MOBA + 2 * hd:OFF_GATE].T
    wg = wb[:, OFF_GATE:]
    wuq3 = w_uq.astype(BF16).reshape(Q_LORA, H, MLA_NOPE + MLA_ROPE)
    wuqn = wuq3[:, :, :MLA_NOPE].reshape(Q_LORA, H * MLA_NOPE)
    wuqr = wuq3[:, :, MLA_NOPE:].reshape(Q_LORA, H * MLA_ROPE)
    wukv3 = w_ukv.astype(BF16).reshape(KV_LORA, H, MLA_NOPE + MLA_V)
    wuk = wukv3[:, :, :MLA_NOPE].reshape(KV_LORA, H * MLA_NOPE)
    wuv = wukv3[:, :, MLA_NOPE:].reshape(KV_LORA, H * MLA_V).T
    cosa, sina = _rope_tables(S, MLA_ROPE)
    cosb, sinb = _rope_tables(S, MOBA_HD)

    row2 = lambda v: v.reshape(1, -1)
    tm = TM_PROJ
    head_out = lambda w, dt: jax.ShapeDtypeStruct((B, H, S, w), dt)
    head_spec = lambda w: pl.BlockSpec((1, H, tm, w), lambda b, s: (b, 0, s, 0))
    tab_spec = pl.BlockSpec((tm, LANES), lambda b, s: (s, 0))
    sub = T // tm
    vt_out = lambda w: jax.ShapeDtypeStruct((B, H, nq, w, T), BF16)
    vt_spec = lambda w: pl.BlockSpec((1, H, 1, w, tm), lambda b, s: (b, 0, s // sub, 0, s % sub))
    weights = [row2(attn_norm), wq, wkv, wpe, wmq, wmk, wmv, wg, row2(b_gate), row2(q_norm),
               wuqn, wuqr, row2(kv_norm), wuk, wuv]
    qa, ka, va, qb, kb, vb, g = pl.pallas_call(
        _proj_kernel,
        grid=(B, S // tm),
        in_specs=[pl.BlockSpec((1, tm, D), lambda b, s: (b, s, 0))]
        + [_const_spec(w.shape) for w in weights] + [tab_spec] * 4,
        out_specs=[head_spec(MLA_QK), head_spec(MLA_QK), vt_spec(MLA_V),
                   head_spec(MOBA_HD), head_spec(MOBA_HD), vt_spec(MOBA_HD),
                   pl.BlockSpec((1, tm, N_BRANCH * D), lambda b, s: (b, s, 0))],
        out_shape=[head_out(MLA_QK, BF16), head_out(MLA_QK, BF16), vt_out(MLA_V),
                   head_out(MOBA_HD, F32), head_out(MOBA_HD, F32), vt_out(MOBA_HD),
                   jax.ShapeDtypeStruct((B, S, N_BRANCH * D), BF16)],
        compiler_params=_params(2),
        name="proj",
    )(x, *weights, cosa, sina, cosb, sinb)

    seq_spec = lambda w: pl.BlockSpec((1, 1, S, w), lambda b, h: (b, h, 0, 0))
    vt_in = lambda w: pl.BlockSpec((1, 1, nq, w, T), lambda b, h: (b, h, 0, 0, 0))
    att_out = pl.BlockSpec((1, S, LANES), lambda b, h: (b, 0, h))
    ya = pl.pallas_call(
        functools.partial(_mla_kernel, nq=nq, T=T),
        grid=(B, H),
        in_specs=[seq_spec(MLA_QK), seq_spec(MLA_QK), vt_in(MLA_V)],
        out_specs=att_out,
        out_shape=jax.ShapeDtypeStruct((B, S, H * MLA_V), BF16),
        scratch_shapes=_flash_scratch(nq, T, MLA_V),
        compiler_params=_params(2),
        name="mla_attn",
    )(qa, ka, va)

    yb = pl.pallas_call(
        functools.partial(_moba_kernel, nq=nq, T=T, nb=nb),
        grid=(B, H),
        in_specs=[seq_spec(MOBA_HD), seq_spec(MOBA_HD), vt_in(MOBA_HD)],
        out_specs=att_out,
        out_shape=jax.ShapeDtypeStruct((B, S, H * MOBA_HD), BF16),
        scratch_shapes=[pltpu.VMEM((S, 2 * LANES), BF16), pltpu.VMEM((S, 2 * LANES), BF16),
                        pltpu.VMEM((nb, LANES), F32)] + _flash_scratch(nq, T, MOBA_HD),
        compiler_params=_params(2),
        name="moba_attn",
    )(qb, kb, vb)

    tm = TM_MIX
    rows = lambda w: pl.BlockSpec((tm, w), lambda i: (i, 0))
    mix_w = [w_o_mla.astype(BF16), w_o_moba.astype(BF16), w_out.astype(BF16)]
    x1 = pl.pallas_call(
        _mix_kernel,
        grid=(B * S // tm,),
        in_specs=[rows(D), rows(D), rows(D), rows(N_BRANCH * D)]
        + [_const_spec(w.shape) for w in mix_w],
        out_specs=rows(D),
        out_shape=jax.ShapeDtypeStruct((B * S, D), F32),
        compiler_params=_params(1),
        name="mix",
    )(x.reshape(B * S, D), ya.reshape(B * S, D), yb.reshape(B * S, D),
      g.reshape(B * S, N_BRANCH * D), *mix_w)

    tm = TM_FFN
    n_chunks = D_FF // FF_CHUNK
    ffn_w = [row2(ffn_norm), w_up.astype(BF16), conv_w, row2(conv_b), w_down.astype(BF16),
             row2(out_gain)]
    out = pl.pallas_call(
        functools.partial(_ffn_kernel, tm=tm, n_chunks=n_chunks),
        grid=(B, S // tm),
        in_specs=[pl.BlockSpec((1, tm, D), lambda b, s: (b, s, 0))]
        + [_const_spec(w.shape) for w in ffn_w],
        out_specs=pl.BlockSpec((1, tm, D), lambda b, s: (b, s, 0)),
        out_shape=jax.ShapeDtypeStruct((B, S, D), F32),
        scratch_shapes=[pltpu.VMEM((2 * n_chunks, SUBLANES, FF_CHUNK), F32),
                        pltpu.VMEM((tm, D_FF), BF16)]
        + [pltpu.VMEM((tm + SUBLANES, FF_CHUNK), F32)] * 4,
        compiler_params=_params(2),
        name="ffn",
    )(x1.reshape(B, S, D), *ffn_w)
    return out


def kernel(x, attn_norm, w_in, b_gate, q_norm, w_uq, kv_norm, w_ukv, w_o_mla, w_o_moba,
           w_out, ffn_norm, w_up, conv_w, conv_b, w_down, final_norm):
    depth = w_in.shape[0]
    assert depth == 1
    return _layer(x, attn_norm[0], w_in[0], b_gate[0], q_norm[0], w_uq[0], kv_norm[0],
                  w_ukv[0], w_o_mla[0], w_o_moba[0], w_out[0], ffn_norm[0], w_up[0],
                  conv_w[0], conv_b[0], w_down[0], final_norm)
```

```python
import functools

import jax
import jax.numpy as jnp
from jax import lax
from jax.experimental import pallas as pl
from jax.experimental.pallas import tpu as pltpu

F32 = jnp.float32
BF16 = jnp.bfloat16

D_MODEL = 1024
MLA_HEADS = 8
MLA_NOPE = 128
MLA_ROPE = 64
MLA_V = 128
Q_LORA = 384
KV_LORA = 256
MOBA_HEADS = 8
MOBA_HD = 128
MOBA_BLOCK = 256
MOBA_TOPK = 3
D_FF = 2816
CONV_W = 3
ROPE_THETA = 10000.0
EPS = 1e-6
NEG = -1e30
N_BRANCH = 2
LOG2E = 1.4426950408889634

OFF_QLAT = 0
OFF_KVLAT = OFF_QLAT + Q_LORA
OFF_KPE = OFF_KVLAT + KV_LORA
OFF_MOBA = OFF_KPE + MLA_ROPE
OFF_GATE = OFF_MOBA + 3 * MOBA_HEADS * MOBA_HD

LANES = 128
SUBLANES = 8
BF16_ROWS = 16
MLA_QK = 2 * LANES
VMEM_LIMIT = 56 * 1024 * 1024

TM_PROJ = 512
ATT_TILE = 1024
QCOLS = 256
GATE_TILE = 2048
DV_AUG = LANES + BF16_ROWS
TM_MIX = 512
TM_FFN = 512
FF_CHUNK = 256


def _dot(a, b):
    return jnp.dot(a, b, preferred_element_type=F32)


def _dot_nt(a, b):
    return lax.dot_general(a, b, (((1,), (1,)), ((), ())), preferred_element_type=F32)


def _rms(x, g):
    return x * lax.rsqrt(jnp.mean(x * x, axis=-1, keepdims=True) + EPS) * g


def _const_spec(shape):
    n = len(shape)
    return pl.BlockSpec(shape, lambda *_: (0,) * n, pipeline_mode=pl.Buffered(1))


def _proj_kernel(x_ref, an_ref, wq_ref, wkv_ref, wpe_ref, wmq_ref, wmk_ref, wmv_ref, wg_ref,
                 bg_ref, qn_ref, wuqn_ref, wuqr_ref, kvn_ref, wuk_ref, wuv_ref,
                 cosa_ref, sina_ref, cosb_ref, sinb_ref,
                 qa_ref, ka_ref, va_ref, qb_ref, kb_ref, vb_ref, g_ref):
    hb = _rms(x_ref[0], an_ref[...]).astype(BF16)
    lane = lax.broadcasted_iota(jnp.int32, (hb.shape[0], LANES), 1)
    low_half = lane < MLA_ROPE
    first32 = (lane % MLA_ROPE) < (MLA_ROPE // 2)
    cosa, sina = cosa_ref[...], sina_ref[...]
    cosb, sinb = cosb_ref[...], sinb_ref[...]
    ones_rows = jnp.ones((DV_AUG - LANES, hb.shape[0]), BF16)

    def rope64(c):
        partner = jnp.where(first32, pltpu.roll(c, LANES - 32, 1), pltpu.roll(c, 32, 1))
        return c * cosa + partner * sina

    def rope128(c):
        return c * cosb + pltpu.roll(c, LANES // 2, 1) * sinb

    ql = _rms(_dot(hb, wq_ref[...]), qn_ref[...]).astype(BF16)
    scale_a = (MLA_NOPE + MLA_ROPE) ** -0.5 * LOG2E
    qn = _dot(ql, wuqn_ref[...]) * scale_a
    qr = _dot(ql, wuqr_ref[...]) * scale_a
    for h in range(MLA_HEADS):
        qa_ref[0, h, :, 0:LANES] = qn[:, h * LANES:(h + 1) * LANES].astype(BF16)
    for c in range(MLA_HEADS // 2):
        r = rope64(qr[:, c * LANES:(c + 1) * LANES])
        qa_ref[0, 2 * c, :, LANES:] = jnp.where(low_half, r, 0.0).astype(BF16)
        qa_ref[0, 2 * c + 1, :, LANES:] = jnp.where(
            low_half, pltpu.roll(r, LANES // 2, 1), 0.0).astype(BF16)

    kvl = _rms(_dot(hb, wkv_ref[...]), kvn_ref[...]).astype(BF16)
    kn = _dot(kvl, wuk_ref[...])
    vt = _dot_nt(wuv_ref[...], kvl)
    krot = rope64(_dot(hb, wpe_ref[...])).astype(BF16)
    for h in range(MLA_HEADS):
        ka_ref[0, h, :, 0:LANES] = kn[:, h * LANES:(h + 1) * LANES].astype(BF16)
        ka_ref[0, h, :, LANES:] = krot
        va_ref[0, h, 0, 0:LANES] = vt[h * LANES:(h + 1) * LANES, :].astype(BF16)
        va_ref[0, h, 0, LANES:] = ones_rows

    mq = _dot(hb, wmq_ref[...])
    for h in range(MOBA_HEADS):
        qb_ref[0, h] = rope128(mq[:, h * LANES:(h + 1) * LANES])
    mk = _dot(hb, wmk_ref[...])
    for h in range(MOBA_HEADS):
        kb_ref[0, h] = rope128(mk[:, h * LANES:(h + 1) * LANES])
    mvt = _dot_nt(wmv_ref[...], hb)
    for h in range(MOBA_HEADS):
        vb_ref[0, h, 0, 0:LANES] = mvt[h * LANES:(h + 1) * LANES, :].astype(BF16)
        vb_ref[0, h, 0, LANES:] = ones_rows

    z = _dot(hb, wg_ref[...]) + bg_ref[...]
    g_ref[0] = (1.0 / (1.0 + jnp.exp(-z))).astype(BF16)


def _flash_causal(nq, T, q_rows, k_tile, vt_tile, o_ref, scr):
    s_refs, x_refs, p_refs, a_refs = scr[0:2], scr[2:4], scr[4:6], scr[6:8]
    m_all, acc_all = scr[8:10]

    groups = range(T // QCOLS)
    cols = lambda c: slice(c * QCOLS, (c + 1) * QCOLS)

    def keys(diag, c):
        return (c + 1) * QCOLS if diag else T

    def scores(i, kt, slot, diag, c):
        n = keys(diag, c)
        s = _dot_nt(k_tile(kt, n), q_rows(i * T + c * QCOLS))
        if diag:
            key = lax.broadcasted_iota(jnp.int32, (n, QCOLS), 0)
            qry = lax.broadcasted_iota(jnp.int32, (n, QCOLS), 1) + c * QCOLS
            s = jnp.where(key <= qry, s, NEG)
        s_refs[slot][0:n, cols(c)] = s
        x_refs[slot][:, cols(c)] = jnp.broadcast_to(jnp.max(s, axis=0, keepdims=True),
                                                    (SUBLANES, QCOLS))

    def softmax(i, slot, diag, c):
        if diag:
            m_new = x_refs[slot][:, cols(c)]
        else:
            m_old = m_all[i, :, cols(c)]
            m_new = jnp.maximum(m_old, x_refs[slot][:, cols(c)])
            alpha = jnp.exp2(m_old - m_new)
            a_refs[slot][:, cols(c)] = alpha
        for r in range(keys(diag, c) // BF16_ROWS):
            rows = slice(r * BF16_ROWS, (r + 1) * BF16_ROWS)
            p = jnp.exp2(s_refs[slot][rows, cols(c)] - m_new[0:1, :])
            p_refs[slot][rows, cols(c)] = p.astype(BF16)
        m_all[i, :, cols(c)] = m_new

    def pv(i, kt, slot, diag, c):
        n = keys(diag, c)
        r = _dot(vt_tile(kt, n), p_refs[slot][0:n, cols(c)])
        acc_all[i, :, cols(c)] = (r if diag else
                                  a_refs[slot][0:1, cols(c)] * acc_all[i, :, cols(c)] + r)

    def run_steps(n_steps, step, carry):
        def body(n, carry):
            return lax.cond(n % 2 == 1, lambda c: step(n, 1, 0, c), lambda c: step(n, 0, 1, c),
                            carry)

        return lax.fori_loop(1, n_steps + 1, body, carry)

    for c in groups:
        scores(0, 0, 0, True, c)
    for c in groups:
        scores(1, 1, 1, True, c)
        softmax(0, 0, True, c)

    def diag_step(n, cur, prv, carry):
        nxt = jnp.minimum(n + 1, nq - 1)
        for c in groups:
            scores(nxt, nxt, prv, True, c)
            softmax(n, cur, True, c)
            pv(n - 1, n - 1, prv, True, c)
        return carry

    run_steps(nq - 1, diag_step, 0)
    for c in groups:
        pv(nq - 1, nq - 1, (nq - 1) % 2, True, c)

    n_items = nq * (nq - 1) // 2

    def advance(i, kt):
        wrap = kt + 1 >= i
        return jnp.where(wrap, i + 1, i), jnp.where(wrap, 0, kt + 1)

    for c in groups:
        scores(1, 0, 0, False, c)
    for c in groups:
        scores(2, 0, 1, False, c)
        softmax(1, 0, False, c)

    def full_step(n, cur, prv, carry):
        ip, kp, ic, kc = carry
        i_n, k_n = advance(ic, kc)
        i_n = jnp.minimum(i_n, nq - 1)
        k_n = jnp.minimum(k_n, nq - 2)
        for c in groups:
            scores(i_n, k_n, prv, False, c)
            softmax(ic, cur, False, c)
            pv(ip, kp, prv, False, c)
        return ic, kc, i_n, k_n

    ip, kp, _, _ = run_steps(n_items - 1, full_step,
                             (jnp.int32(1), jnp.int32(0), jnp.int32(2), jnp.int32(0)))
    for c in groups:
        pv(ip, kp, (n_items - 1) % 2, False, c)

    def finish(i, c):
        acc = acc_all[i]
        o = acc[0:LANES] / acc[LANES:LANES + 1]
        o_ref[0, pl.ds(pl.multiple_of(i * T, T), T), :] = o.T.astype(BF16)
        return c

    lax.fori_loop(0, nq, finish, 0)


def _flash_scratch(nq, T, dv):
    return ([pltpu.VMEM((T, T), F32)] * 2 + [pltpu.VMEM((SUBLANES, T), F32)] * 2
            + [pltpu.VMEM((T, T), BF16)] * 2 + [pltpu.VMEM((SUBLANES, T), F32)] * 2
            + [pltpu.VMEM((nq, SUBLANES, T), F32), pltpu.VMEM((nq, dv, T), F32)])


def _q_rows(ref):
    return lambda start: ref[pl.ds(pl.multiple_of(start, QCOLS), QCOLS), :]


def _k_tile(ref, T):
    return lambda kt, n: ref[pl.ds(pl.multiple_of(kt * T, T), n), :]


def _vt_tile(ref):
    return lambda kt, n: ref[0, 0, kt, :, 0:n]


def _mla_kernel(q_ref, k_ref, vt_ref, o_ref, *scr, nq, T):
    _flash_causal(nq, T, _q_rows(q_ref.at[0, 0]), _k_tile(k_ref.at[0, 0], T),
                  _vt_tile(vt_ref), o_ref, scr)


def _moba_kernel(q_ref, k_ref, vt_ref, o_ref, kaug_ref, qaug_ref, kmean_ref, *scr, nq, T, nb):
    lane = lax.broadcasted_iota(jnp.int32, (MOBA_BLOCK, LANES), 1)

    def blk_body(b, c):
        off = pl.multiple_of(b * MOBA_BLOCK, MOBA_BLOCK)
        kb = k_ref[0, 0, pl.ds(off, MOBA_BLOCK), :]
        kmean_ref[pl.ds(b, 1), :] = jnp.mean(kb, axis=0, keepdims=True)
        kaug_ref[pl.ds(off, MOBA_BLOCK), 0:LANES] = kb.astype(BF16)
        kaug_ref[pl.ds(off, MOBA_BLOCK), LANES:] = jnp.where(lane == b, 1.0, 0.0).astype(BF16)
        return c

    lax.fori_loop(0, nb, blk_body, 0)

    gt_n = GATE_TILE
    kblk = lax.broadcasted_iota(jnp.int32, (nb, gt_n), 0)
    kblk_f = kblk.astype(F32)
    qoff = lax.broadcasted_iota(jnp.int32, (nb, gt_n), 1) // MOBA_BLOCK

    def gate_body(t, c):
        off = pl.multiple_of(t * gt_n, gt_n)
        q = q_ref[0, 0, pl.ds(off, gt_n), :]
        gt = lax.dot_general(kmean_ref[...], q, (((1,), (1,)), ((), ())),
                             precision=lax.Precision.HIGHEST, preferred_element_type=F32)
        qblk = t * (gt_n // MOBA_BLOCK) + qoff
        past = kblk < qblk
        g = jnp.where(past, gt, -jnp.inf)
        sel = kblk == qblk
        for _ in range(MOBA_TOPK):
            first = jnp.min(jnp.where(g == jnp.max(g, axis=0, keepdims=True), kblk_f, float(nb)),
                            axis=0, keepdims=True)
            hit = kblk_f == first
            sel = sel | (hit & past)
            g = jnp.where(hit, -jnp.inf, g)
        keep = sel
        bias_t = jnp.where(keep, 0.0, NEG)
        bias_t = jnp.concatenate([bias_t, jnp.zeros((LANES - nb, gt_n), F32)], axis=0)
        qaug_ref[pl.ds(off, gt_n), 0:LANES] = (q * (MOBA_HD ** -0.5 * LOG2E)).astype(BF16)
        qaug_ref[pl.ds(off, gt_n), LANES:] = bias_t.T.astype(BF16)
        return c

    lax.fori_loop(0, nq * T // gt_n, gate_body, 0)

    _flash_causal(nq, T, _q_rows(qaug_ref), _k_tile(kaug_ref, T), _vt_tile(vt_ref), o_ref, scr)


def _mix_kernel(x_ref, ya_ref, yb_ref, g_ref, woa_ref, wob_ref, wout_ref, o_ref):
    a = _dot(ya_ref[...], woa_ref[...])
    b = _dot(yb_ref[...], wob_ref[...])
    g = g_ref[...].astype(F32)
    mix = g[:, :D_MODEL] * a + g[:, D_MODEL:] * b
    o_ref[...] = x_ref[...] + _dot(mix.astype(BF16), wout_ref[...])


def _ffn_kernel(x_ref, fn_ref, wup_ref, cw_ref, cb_ref, wdn_ref, gn_ref, o_ref, carry_ref,
                act_ref, *bufs, tm, n_chunks):
    si = pl.program_id(1)

    @pl.when(si == 0)
    def _():
        carry_ref[...] = jnp.zeros_like(carry_ref)

    x = x_ref[0]
    hb = _rms(x, fn_ref[...]).astype(BF16)

    def conv(u, buf, idx, col0):
        buf[0:SUBLANES, :] = carry_ref[idx]
        buf[SUBLANES:, :] = u
        carry_ref[idx] = u[tm - SUBLANES:, :]
        u1 = buf[SUBLANES - 1:SUBLANES - 1 + tm, :]
        u2 = buf[SUBLANES - 2:SUBLANES - 2 + tm, :]
        w = cw_ref[:, col0:col0 + FF_CHUNK]
        return cb_ref[:, col0:col0 + FF_CHUNK] + w[0:1] * u2 + w[1:2] * u1 + w[2:3] * u

    for c in range(n_chunks):
        ca = c * FF_CHUNK
        cb = D_FF + c * FF_CHUNK
        ya = conv(_dot(hb, wup_ref[:, ca:ca + FF_CHUNK]), bufs[2 * (c % 2)], c, ca)
        yb = conv(_dot(hb, wup_ref[:, cb:cb + FF_CHUNK]), bufs[2 * (c % 2) + 1], n_chunks + c, cb)
        act_ref[:, ca:ca + FF_CHUNK] = (ya * (1.0 / (1.0 + jnp.exp(-ya))) * yb).astype(BF16)
    o_ref[0] = _rms(x + _dot(act_ref[...], wdn_ref[...]), gn_ref[...])


def _rope_tables(seq, d):
    inv = ROPE_THETA ** (-jnp.arange(0, d, 2, dtype=F32) / d)
    ang = jnp.arange(seq).astype(F32)[:, None] * inv[None, :]
    cos, sin = jnp.cos(ang), jnp.sin(ang)
    reps = LANES // d
    return (jnp.tile(jnp.concatenate([cos, cos], axis=-1), (1, reps)),
            jnp.tile(jnp.concatenate([-sin, sin], axis=-1), (1, reps)))


def _params(n_axes):
    return pltpu.CompilerParams(dimension_semantics=("arbitrary",) * n_axes,
                                vmem_limit_bytes=VMEM_LIMIT)


def _layer(x, attn_norm, w_in, b_gate, q_norm, w_uq, kv_norm, w_ukv, w_o_mla, w_o_moba,
           w_out, ffn_norm, w_up, conv_w, conv_b, w_down, out_gain):
    B, S, D = x.shape
    H = MLA_HEADS
    T = ATT_TILE
    assert D == D_MODEL and S % T == 0 and S // T >= 3 and S % GATE_TILE == 0
    assert D_FF % FF_CHUNK == 0 and GATE_TILE % MOBA_BLOCK == 0 and T % TM_PROJ == 0
    nq = S // T
    nb = S // MOBA_BLOCK
    assert nb <= LANES and nb % SUBLANES == 0

    wcols = lambda a, b: w_in[:, a:b].astype(BF16)
    wq = wcols(OFF_QLAT, OFF_KVLAT)
    wkv = wcols(OFF_KVLAT, OFF_KPE)
    wpe = jnp.pad(wcols(OFF_KPE, OFF_MOBA), ((0, 0), (0, LANES - MLA_ROPE)))
    hd = MOBA_HEADS * MOBA_HD
    wmq = wcols(OFF_MOBA, OFF_MOBA + hd)
    wmk = wcols(OFF_MOBA + hd, OFF_MOBA + 2 * hd)
    wmv = wcols(OFF_MOBA + 2 * hd, OFF_GATE).T
    wg = wcols(OFF_GATE, OFF_GATE + N_BRANCH * D_MODEL)
    wuq3 = w_uq.astype(BF16).reshape(Q_LORA, H, MLA_NOPE + MLA_ROPE)
    wuqn = wuq3[:, :, :MLA_NOPE].reshape(Q_LORA, H * MLA_NOPE)
    wuqr = wuq3[:, :, MLA_NOPE:].reshape(Q_LORA, H * MLA_ROPE)
    wukv3 = w_ukv.astype(BF16).reshape(KV_LORA, H, MLA_NOPE + MLA_V)
    wuk = wukv3[:, :, :MLA_NOPE].reshape(KV_LORA, H * MLA_NOPE)
    wuv = wukv3[:, :, MLA_NOPE:].reshape(KV_LORA, H * MLA_V).T
    cosa, sina = _rope_tables(S, MLA_ROPE)
    cosb, sinb = _rope_tables(S, MOBA_HD)

    row2 = lambda v: v.reshape(1, -1)
    tm = TM_PROJ
    head_out = lambda w, dt: jax.ShapeDtypeStruct((B, H, S, w), dt)
    head_spec = lambda w: pl.BlockSpec((1, H, tm, w), lambda b, s: (b, 0, s, 0))
    tab_spec = pl.BlockSpec((tm, LANES), lambda b, s: (s, 0))
    sub = T // tm
    assert MLA_V == LANES and MOBA_HD == LANES
    vt_out = lambda w: jax.ShapeDtypeStruct((B, H, nq, DV_AUG, T), BF16)
    vt_spec = lambda w: pl.BlockSpec((1, H, 1, DV_AUG, tm),
                                     lambda b, s: (b, 0, s // sub, 0, s % sub))
    weights = [row2(attn_norm), wq, wkv, wpe, wmq, wmk, wmv, wg, row2(b_gate), row2(q_norm),
               wuqn, wuqr, row2(kv_norm), wuk, wuv]
    qa, ka, va, qb, kb, vb, g = pl.pallas_call(
        _proj_kernel,
        grid=(B, S // tm),
        in_specs=[pl.BlockSpec((1, tm, D), lambda b, s: (b, s, 0))]
        + [_const_spec(w.shape) for w in weights] + [tab_spec] * 4,
        out_specs=[head_spec(MLA_QK), head_spec(MLA_QK), vt_spec(MLA_V),
                   head_spec(MOBA_HD), head_spec(MOBA_HD), vt_spec(MOBA_HD),
                   pl.BlockSpec((1, tm, N_BRANCH * D), lambda b, s: (b, s, 0))],
        out_shape=[head_out(MLA_QK, BF16), head_out(MLA_QK, BF16), vt_out(MLA_V),
                   head_out(MOBA_HD, F32), head_out(MOBA_HD, F32), vt_out(MOBA_HD),
                   jax.ShapeDtypeStruct((B, S, N_BRANCH * D), BF16)],
        compiler_params=_params(2),
        name="proj",
    )(x, *weights, cosa, sina, cosb, sinb)

    seq_spec = lambda w: pl.BlockSpec((1, 1, S, w), lambda b, h: (b, h, 0, 0))
    vt_in = lambda w: pl.BlockSpec((1, 1, nq, DV_AUG, T), lambda b, h: (b, h, 0, 0, 0))
    att_out = pl.BlockSpec((1, S, LANES), lambda b, h: (b, 0, h))
    ya = pl.pallas_call(
        functools.partial(_mla_kernel, nq=nq, T=T),
        grid=(B, H),
        in_specs=[seq_spec(MLA_QK), seq_spec(MLA_QK), vt_in(MLA_V)],
        out_specs=att_out,
        out_shape=jax.ShapeDtypeStruct((B, S, H * MLA_V), BF16),
        scratch_shapes=_flash_scratch(nq, T, DV_AUG),
        compiler_params=_params(2),
        name="mla_attn",
    )(qa, ka, va)

    yb = pl.pallas_call(
        functools.partial(_moba_kernel, nq=nq, T=T, nb=nb),
        grid=(B, H),
        in_specs=[seq_spec(MOBA_HD), seq_spec(MOBA_HD), vt_in(MOBA_HD)],
        out_specs=att_out,
        out_shape=jax.ShapeDtypeStruct((B, S, H * MOBA_HD), BF16),
        scratch_shapes=[pltpu.VMEM((S, 2 * LANES), BF16), pltpu.VMEM((S, 2 * LANES), BF16),
                        pltpu.VMEM((nb, LANES), F32)] + _flash_scratch(nq, T, DV_AUG),
        compiler_params=_params(2),
        name="moba_attn",
    )(qb, kb, vb)

    tm = TM_MIX
    rows = lambda w: pl.BlockSpec((tm, w), lambda i: (i, 0))
    mix_w = [w_o_mla.astype(BF16), w_o_moba.astype(BF16), w_out.astype(BF16)]
    x1 = pl.pallas_call(
        _mix_kernel,
        grid=(B * S // tm,),
        in_specs=[rows(D), rows(D), rows(D), rows(N_BRANCH * D)]
        + [_const_spec(w.shape) for w in mix_w],
        out_specs=rows(D),
        out_shape=jax.ShapeDtypeStruct((B * S, D), F32),
        compiler_params=_params(1),
        name="mix",
    )(x.reshape(B * S, D), ya.reshape(B * S, D), yb.reshape(B * S, D),
      g.reshape(B * S, N_BRANCH * D), *mix_w)

    tm = TM_FFN
    n_chunks = D_FF // FF_CHUNK
    ffn_w = [row2(ffn_norm), w_up.astype(BF16), conv_w, row2(conv_b), w_down.astype(BF16),
             row2(out_gain)]
    out = pl.pallas_call(
        functools.partial(_ffn_kernel, tm=tm, n_chunks=n_chunks),
        grid=(B, S // tm),
        in_specs=[pl.BlockSpec((1, tm, D), lambda b, s: (b, s, 0))]
        + [_const_spec(w.shape) for w in ffn_w],
        out_specs=pl.BlockSpec((1, tm, D), lambda b, s: (b, s, 0)),
        out_shape=jax.ShapeDtypeStruct((B, S, D), F32),
        scratch_shapes=[pltpu.VMEM((2 * n_chunks, SUBLANES, FF_CHUNK), F32),
                        pltpu.VMEM((tm, D_FF), BF16)]
        + [pltpu.VMEM((tm + SUBLANES, FF_CHUNK), F32)] * 4,
        compiler_params=_params(2),
        name="ffn",
    )(x1.reshape(B, S, D), *ffn_w)
    return out


def kernel(x, attn_norm, w_in, b_gate, q_norm, w_uq, kv_norm, w_ukv, w_o_mla, w_o_moba,
           w_out, ffn_norm, w_up, conv_w, conv_b, w_down, final_norm):
    depth = w_in.shape[0]
    assert depth == 1
    return _layer(x, attn_norm[0], w_in[0], b_gate[0], q_norm[0], w_uq[0], kv_norm[0],
                  w_ukv[0], w_o_mla[0], w_o_moba[0], w_out[0], ffn_norm[0], w_up[0],
                  conv_w[0], conv_b[0], w_down[0], final_norm)
```

```python
import functools

import jax
import jax.numpy as jnp
from jax import lax
from jax.experimental import pallas as pl
from jax.experimental.pallas import tpu as pltpu

F32 = jnp.float32
BF16 = jnp.bfloat16

D_MODEL = 1024
MLA_HEADS = 8
MLA_NOPE = 128
MLA_ROPE = 64
MLA_V = 128
Q_LORA = 384
KV_LORA = 256
MOBA_HEADS = 8
MOBA_HD = 128
MOBA_BLOCK = 256
MOBA_TOPK = 3
D_FF = 2816
CONV_W = 3
ROPE_THETA = 10000.0
EPS = 1e-6
NEG = -1e30
N_BRANCH = 2
LOG2E = 1.4426950408889634

OFF_QLAT = 0
OFF_KVLAT = OFF_QLAT + Q_LORA
OFF_KPE = OFF_KVLAT + KV_LORA
OFF_MOBA = OFF_KPE + MLA_ROPE
OFF_GATE = OFF_MOBA + 3 * MOBA_HEADS * MOBA_HD

LANES = 128
SUBLANES = 8
BF16_ROWS = 16
MLA_QK = 2 * LANES
VMEM_LIMIT = 56 * 1024 * 1024

TM_PROJ = 512
ATT_TILE = 1024
QCOLS = 256
GATE_TILE = 2048
DV_AUG = LANES + BF16_ROWS
TM_FFN = 512
FF_CHUNK = 256


def _dot(a, b):
    return jnp.dot(a, b, preferred_element_type=F32)


def _dot_nt(a, b):
    return lax.dot_general(a, b, (((1,), (1,)), ((), ())), preferred_element_type=F32)


def _rms(x, g):
    return x * lax.rsqrt(jnp.mean(x * x, axis=-1, keepdims=True) + EPS) * g


def _const_spec(shape):
    n = len(shape)
    return pl.BlockSpec(shape, lambda *_: (0,) * n, pipeline_mode=pl.Buffered(1))


def _proj_kernel(x_ref, an_ref, wq_ref, wkv_ref, wpe_ref, wmq_ref, wmk_ref, wmv_ref, wg_ref,
                 bg_ref, qn_ref, wuqn_ref, wuqr_ref, kvn_ref, wuk_ref, wuv_ref,
                 cra_ref, sra_ref, cta_ref, sta_ref, sga_ref,
                 crb_ref, srb_ref, ctb_ref, stb_ref, sgb_ref,
                 qa_ref, ka_ref, va_ref, qb_ref, kb_ref, vb_ref, g_ref):
    hb = _rms(x_ref[0], an_ref[...]).astype(BF16)
    lane = lax.broadcasted_iota(jnp.int32, (hb.shape[0], LANES), 1)
    low_half = lane < MLA_ROPE
    first32 = (lane % MLA_ROPE) < (MLA_ROPE // 2)

    def rope_table(refs):
        cr_ref, sr_ref, ct_ref, st_ref, sign_ref = refs
        tile = pl.ds(pl.program_id(1), 1)
        cr, sr, ct, st = cr_ref[...], sr_ref[...], ct_ref[tile, :], st_ref[tile, :]
        return ct * cr - st * sr, (st * cr + ct * sr) * sign_ref[...]

    cosa, sina = rope_table((cra_ref, sra_ref, cta_ref, sta_ref, sga_ref))
    cosb, sinb = rope_table((crb_ref, srb_ref, ctb_ref, stb_ref, sgb_ref))
    ones_rows = jnp.ones((DV_AUG - LANES, hb.shape[0]), BF16)

    def rope64(c):
        partner = jnp.where(first32, pltpu.roll(c, LANES - 32, 1), pltpu.roll(c, 32, 1))
        return c * cosa + partner * sina

    def rope128(c):
        return c * cosb + pltpu.roll(c, LANES // 2, 1) * sinb

    ql = _rms(_dot(hb, wq_ref[...]), qn_ref[...]).astype(BF16)
    scale_a = (MLA_NOPE + MLA_ROPE) ** -0.5 * LOG2E
    qn = _dot(ql, wuqn_ref[...]) * scale_a
    qr = _dot(ql, wuqr_ref[...]) * scale_a
    for h in range(MLA_HEADS):
        qa_ref[0, h, :, 0:LANES] = qn[:, h * LANES:(h + 1) * LANES].astype(BF16)
    for c in range(MLA_HEADS // 2):
        r = rope64(qr[:, c * LANES:(c + 1) * LANES])
        qa_ref[0, 2 * c, :, LANES:] = jnp.where(low_half, r, 0.0).astype(BF16)
        qa_ref[0, 2 * c + 1, :, LANES:] = jnp.where(
            low_half, pltpu.roll(r, LANES // 2, 1), 0.0).astype(BF16)

    kvl = _rms(_dot(hb, wkv_ref[...]), kvn_ref[...]).astype(BF16)
    kn = _dot(kvl, wuk_ref[...])
    vt = _dot_nt(wuv_ref[...], kvl)
    krot = rope64(_dot(hb, wpe_ref[...])).astype(BF16)
    for h in range(MLA_HEADS):
        ka_ref[0, h, :, 0:LANES] = kn[:, h * LANES:(h + 1) * LANES].astype(BF16)
        ka_ref[0, h, :, LANES:] = krot
        va_ref[0, h, 0, 0:LANES] = vt[h * LANES:(h + 1) * LANES, :].astype(BF16)
        va_ref[0, h, 0, LANES:] = ones_rows

    mq = _dot(hb, wmq_ref[...])
    for h in range(MOBA_HEADS):
        qb_ref[0, h] = rope128(mq[:, h * LANES:(h + 1) * LANES])
    mk = _dot(hb, wmk_ref[...])
    for h in range(MOBA_HEADS):
        kb_ref[0, h] = rope128(mk[:, h * LANES:(h + 1) * LANES])
    mvt = _dot_nt(wmv_ref[...], hb)
    for h in range(MOBA_HEADS):
        vb_ref[0, h, 0, 0:LANES] = mvt[h * LANES:(h + 1) * LANES, :].astype(BF16)
        vb_ref[0, h, 0, LANES:] = ones_rows

    z = _dot(hb, wg_ref[...]) + bg_ref[...]
    g_ref[0] = (1.0 / (1.0 + jnp.exp(-z))).astype(BF16)


def _flash_causal(nq, T, q_rows, k_tile, vt_tile, o_ref, scr):
    s_refs, x_refs, p_refs, a_refs = scr[0:2], scr[2:4], scr[4:6], scr[6:8]
    m_all, acc_all = scr[8:10]

    groups = range(T // QCOLS)
    cols = lambda c: slice(c * QCOLS, (c + 1) * QCOLS)

    def keys(diag, c):
        return (c + 1) * QCOLS if diag else T

    def scores(i, kt, slot, diag, c):
        n = keys(diag, c)
        s = _dot_nt(k_tile(kt, n), q_rows(i * T + c * QCOLS))
        if diag:
            key = lax.broadcasted_iota(jnp.int32, (n, QCOLS), 0)
            qry = lax.broadcasted_iota(jnp.int32, (n, QCOLS), 1) + c * QCOLS
            s = jnp.where(key <= qry, s, NEG)
        s_refs[slot][0:n, cols(c)] = s
        x_refs[slot][:, cols(c)] = jnp.broadcast_to(jnp.max(s, axis=0, keepdims=True),
                                                    (SUBLANES, QCOLS))

    def softmax(i, slot, diag, c):
        if diag:
            m_new = x_refs[slot][:, cols(c)]
        else:
            m_old = m_all[i, :, cols(c)]
            m_new = jnp.maximum(m_old, x_refs[slot][:, cols(c)])
            alpha = jnp.exp2(m_old - m_new)
            a_refs[slot][:, cols(c)] = alpha
        for r in range(keys(diag, c) // BF16_ROWS):
            rows = slice(r * BF16_ROWS, (r + 1) * BF16_ROWS)
            p = jnp.exp2(s_refs[slot][rows, cols(c)] - m_new[0:1, :])
            p_refs[slot][rows, cols(c)] = p.astype(BF16)
        m_all[i, :, cols(c)] = m_new

    def pv(i, kt, slot, diag, c):
        n = keys(diag, c)
        r = _dot(vt_tile(kt, n), p_refs[slot][0:n, cols(c)])
        acc_all[i, :, cols(c)] = (r if diag else
                                  a_refs[slot][0:1, cols(c)] * acc_all[i, :, cols(c)] + r)

    def run_steps(n_steps, step, carry):
        def body(n, carry):
            return lax.cond(n % 2 == 1, lambda c: step(n, 1, 0, c), lambda c: step(n, 0, 1, c),
                            carry)

        return lax.fori_loop(1, n_steps + 1, body, carry)

    for c in groups:
        scores(0, 0, 0, True, c)
    for c in groups:
        scores(1, 1, 1, True, c)
        softmax(0, 0, True, c)

    def diag_step(n, cur, prv, carry):
        nxt = jnp.minimum(n + 1, nq - 1)
        for c in groups:
            scores(nxt, nxt, prv, True, c)
            softmax(n, cur, True, c)
            pv(n - 1, n - 1, prv, True, c)
        return carry

    run_steps(nq - 1, diag_step, 0)
    for c in groups:
        pv(nq - 1, nq - 1, (nq - 1) % 2, True, c)

    n_items = nq * (nq - 1) // 2

    def advance(i, kt):
        wrap = kt + 1 >= i
        return jnp.where(wrap, i + 1, i), jnp.where(wrap, 0, kt + 1)

    for c in groups:
        scores(1, 0, 0, False, c)
    for c in groups:
        scores(2, 0, 1, False, c)
        softmax(1, 0, False, c)

    def full_step(n, cur, prv, carry):
        ip, kp, ic, kc = carry
        i_n, k_n = advance(ic, kc)
        i_n = jnp.minimum(i_n, nq - 1)
        k_n = jnp.minimum(k_n, nq - 2)
        for c in groups:
            pv(ip, kp, prv, False, c)
            scores(i_n, k_n, prv, False, c)
            softmax(ic, cur, False, c)
        return ic, kc, i_n, k_n

    ip, kp, _, _ = run_steps(n_items - 1, full_step,
                             (jnp.int32(1), jnp.int32(0), jnp.int32(2), jnp.int32(0)))
    for c in groups:
        pv(ip, kp, (n_items - 1) % 2, False, c)

    def finish(i, c):
        acc = acc_all[i]
        o = acc[0:LANES] / acc[LANES:LANES + 1]
        o_ref[0, pl.ds(pl.multiple_of(i * T, T), T), :] = o.T.astype(BF16)
        return c

    lax.fori_loop(0, nq, finish, 0)


def _flash_scratch(nq, T, dv):
    return ([pltpu.VMEM((T, T), F32)] * 2 + [pltpu.VMEM((SUBLANES, T), F32)] * 2
            + [pltpu.VMEM((T, T), BF16)] * 2 + [pltpu.VMEM((SUBLANES, T), F32)] * 2
            + [pltpu.VMEM((nq, SUBLANES, T), F32), pltpu.VMEM((nq, dv, T), F32)])


def _q_rows(ref):
    return lambda start: ref[pl.ds(pl.multiple_of(start, QCOLS), QCOLS), :]


def _k_tile(ref, T):
    return lambda kt, n: ref[pl.ds(pl.multiple_of(kt * T, T), n), :]


def _vt_tile(ref):
    return lambda kt, n: ref[0, 0, kt, :, 0:n]


def _mla_kernel(q_ref, k_ref, vt_ref, o_ref, *scr, nq, T):
    _flash_causal(nq, T, _q_rows(q_ref.at[0, 0]), _k_tile(k_ref.at[0, 0], T),
                  _vt_tile(vt_ref), o_ref, scr)


def _moba_kernel(q_ref, k_ref, vt_ref, o_ref, kaug_ref, qaug_ref, kmean_ref, *scr, nq, T, nb):
    lane = lax.broadcasted_iota(jnp.int32, (MOBA_BLOCK, LANES), 1)

    def blk_body(b, c):
        off = pl.multiple_of(b * MOBA_BLOCK, MOBA_BLOCK)
        kb = k_ref[0, 0, pl.ds(off, MOBA_BLOCK), :]
        kmean_ref[pl.ds(b, 1), :] = jnp.mean(kb, axis=0, keepdims=True)
        kaug_ref[pl.ds(off, MOBA_BLOCK), 0:LANES] = kb.astype(BF16)
        kaug_ref[pl.ds(off, MOBA_BLOCK), LANES:] = jnp.where(lane == b, 1.0, 0.0).astype(BF16)
        return c

    lax.fori_loop(0, nb, blk_body, 0)

    gt_n = GATE_TILE
    kblk = lax.broadcasted_iota(jnp.int32, (nb, gt_n), 0)
    kblk_f = kblk.astype(F32)
    qoff = lax.broadcasted_iota(jnp.int32, (nb, gt_n), 1) // MOBA_BLOCK

    def gate_body(t, c):
        off = pl.multiple_of(t * gt_n, gt_n)
        q = q_ref[0, 0, pl.ds(off, gt_n), :]
        gt = lax.dot_general(kmean_ref[...], q, (((1,), (1,)), ((), ())),
                             precision=lax.Precision.HIGHEST, preferred_element_type=F32)
        qblk = t * (gt_n // MOBA_BLOCK) + qoff
        past = kblk < qblk
        g = jnp.where(past, gt, -jnp.inf)
        sel = kblk == qblk
        for _ in range(MOBA_TOPK):
            first = jnp.min(jnp.where(g == jnp.max(g, axis=0, keepdims=True), kblk_f, float(nb)),
                            axis=0, keepdims=True)
            hit = kblk_f == first
            sel = sel | (hit & past)
            g = jnp.where(hit, -jnp.inf, g)
        keep = sel
        bias_t = jnp.where(keep, 0.0, NEG)
        bias_t = jnp.concatenate([bias_t, jnp.zeros((LANES - nb, gt_n), F32)], axis=0)
        qaug_ref[pl.ds(off, gt_n), 0:LANES] = (q * (MOBA_HD ** -0.5 * LOG2E)).astype(BF16)
        qaug_ref[pl.ds(off, gt_n), LANES:] = bias_t.T.astype(BF16)
        return c

    lax.fori_loop(0, nq * T // gt_n, gate_body, 0)

    _flash_causal(nq, T, _q_rows(qaug_ref), _k_tile(kaug_ref, T), _vt_tile(vt_ref), o_ref, scr)


def _ffn_kernel(x_ref, ya_ref, yb_ref, g_ref, woa_ref, wob_ref, wout_ref,
                fn_ref, wup_ref, cw_ref, cb_ref, wdn_ref, gn_ref, o_ref, carry_ref,
                act_ref, *bufs, tm, n_chunks):
    si = pl.program_id(1)

    @pl.when(si == 0)
    def _():
        carry_ref[...] = jnp.zeros_like(carry_ref)

    g = g_ref[0].astype(F32)
    mix = (g[:, :D_MODEL] * _dot(ya_ref[0], woa_ref[...])
           + g[:, D_MODEL:] * _dot(yb_ref[0], wob_ref[...]))
    x = x_ref[0] + _dot(mix.astype(BF16), wout_ref[...])
    hb = _rms(x, fn_ref[...]).astype(BF16)

    def conv(u, buf, idx, col0):
        buf[0:SUBLANES, :] = carry_ref[idx]
        buf[SUBLANES:, :] = u
        carry_ref[idx] = u[tm - SUBLANES:, :]
        u1 = buf[SUBLANES - 1:SUBLANES - 1 + tm, :]
        u2 = buf[SUBLANES - 2:SUBLANES - 2 + tm, :]
        w = cw_ref[:, col0:col0 + FF_CHUNK]
        return cb_ref[:, col0:col0 + FF_CHUNK] + w[0:1] * u2 + w[1:2] * u1 + w[2:3] * u

    for c in range(n_chunks):
        ca = c * FF_CHUNK
        cb = D_FF + c * FF_CHUNK
        ya = conv(_dot(hb, wup_ref[:, ca:ca + FF_CHUNK]), bufs[2 * (c % 2)], c, ca)
        yb = conv(_dot(hb, wup_ref[:, cb:cb + FF_CHUNK]), bufs[2 * (c % 2) + 1], n_chunks + c, cb)
        act_ref[:, ca:ca + FF_CHUNK] = (ya * (1.0 / (1.0 + jnp.exp(-ya))) * yb).astype(BF16)
    o_ref[0] = _rms(x + _dot(act_ref[...], wdn_ref[...]), gn_ref[...])


def _rope_tables(seq, d, tm):
    inv = ROPE_THETA ** (-jnp.arange(0, d, 2, dtype=F32) / d)
    lanes = lambda t: jnp.tile(jnp.concatenate([t, t], axis=-1), (1, LANES // d))
    row_ang = jnp.arange(tm).astype(F32)[:, None] * inv[None, :]
    tile_ang = (jnp.arange(seq // tm) * tm).astype(F32)[:, None] * inv[None, :]
    half = jnp.ones((1, d // 2), F32)
    sign = jnp.tile(jnp.concatenate([-half, half], axis=-1), (1, LANES // d))
    return [lanes(jnp.cos(row_ang)), lanes(jnp.sin(row_ang)),
            lanes(jnp.cos(tile_ang)), lanes(jnp.sin(tile_ang)), sign]


def _params(n_axes):
    return pltpu.CompilerParams(dimension_semantics=("arbitrary",) * n_axes,
                                vmem_limit_bytes=VMEM_LIMIT)


def _layer(x, attn_norm, w_in, b_gate, q_norm, w_uq, kv_norm, w_ukv, w_o_mla, w_o_moba,
           w_out, ffn_norm, w_up, conv_w, conv_b, w_down, out_gain):
    B, S, D = x.shape
    H = MLA_HEADS
    T = ATT_TILE
    assert D == D_MODEL and S % T == 0 and S // T >= 3 and S % GATE_TILE == 0
    assert D_FF % FF_CHUNK == 0 and GATE_TILE % MOBA_BLOCK == 0 and T % TM_PROJ == 0
    nq = S // T
    nb = S // MOBA_BLOCK
    assert nb <= LANES and nb % SUBLANES == 0

    wcols = lambda a, b: w_in[:, a:b].astype(BF16)
    wq = wcols(OFF_QLAT, OFF_KVLAT)
    wkv = wcols(OFF_KVLAT, OFF_KPE)
    wpe = jnp.pad(wcols(OFF_KPE, OFF_MOBA), ((0, 0), (0, LANES - MLA_ROPE)))
    hd = MOBA_HEADS * MOBA_HD
    wmq = wcols(OFF_MOBA, OFF_MOBA + hd)
    wmk = wcols(OFF_MOBA + hd, OFF_MOBA + 2 * hd)
    wmv = wcols(OFF_MOBA + 2 * hd, OFF_GATE).T
    wg = wcols(OFF_GATE, OFF_GATE + N_BRANCH * D_MODEL)
    wuq3 = w_uq.astype(BF16).reshape(Q_LORA, H, MLA_NOPE + MLA_ROPE)
    wuqn = wuq3[:, :, :MLA_NOPE].reshape(Q_LORA, H * MLA_NOPE)
    wuqr = wuq3[:, :, MLA_NOPE:].reshape(Q_LORA, H * MLA_ROPE)
    wukv3 = w_ukv.astype(BF16).reshape(KV_LORA, H, MLA_NOPE + MLA_V)
    wuk = wukv3[:, :, :MLA_NOPE].reshape(KV_LORA, H * MLA_NOPE)
    wuv = wukv3[:, :, MLA_NOPE:].reshape(KV_LORA, H * MLA_V).T
    row2 = lambda v: v.reshape(1, -1)
    tm = TM_PROJ
    ropes = _rope_tables(S, MLA_ROPE, tm) + _rope_tables(S, MOBA_HD, tm)
    head_out = lambda w, dt: jax.ShapeDtypeStruct((B, H, S, w), dt)
    head_spec = lambda w: pl.BlockSpec((1, H, tm, w), lambda b, s: (b, 0, s, 0))
    sub = T // tm
    assert MLA_V == LANES and MOBA_HD == LANES
    vt_out = lambda w: jax.ShapeDtypeStruct((B, H, nq, DV_AUG, T), BF16)
    vt_spec = lambda w: pl.BlockSpec((1, H, 1, DV_AUG, tm),
                                     lambda b, s: (b, 0, s // sub, 0, s % sub))
    weights = [row2(attn_norm), wq, wkv, wpe, wmq, wmk, wmv, wg, row2(b_gate), row2(q_norm),
               wuqn, wuqr, row2(kv_norm), wuk, wuv]
    qa, ka, va, qb, kb, vb, g = pl.pallas_call(
        _proj_kernel,
        grid=(B, S // tm),
        in_specs=[pl.BlockSpec((1, tm, D), lambda b, s: (b, s, 0))]
        + [_const_spec(w.shape) for w in weights + ropes],
        out_specs=[head_spec(MLA_QK), head_spec(MLA_QK), vt_spec(MLA_V),
                   head_spec(MOBA_HD), head_spec(MOBA_HD), vt_spec(MOBA_HD),
                   pl.BlockSpec((1, tm, N_BRANCH * D), lambda b, s: (b, s, 0))],
        out_shape=[head_out(MLA_QK, BF16), head_out(MLA_QK, BF16), vt_out(MLA_V),
                   head_out(MOBA_HD, F32), head_out(MOBA_HD, F32), vt_out(MOBA_HD),
                   jax.ShapeDtypeStruct((B, S, N_BRANCH * D), BF16)],
        compiler_params=_params(2),
        name="proj",
    )(x, *weights, *ropes)

    seq_spec = lambda w: pl.BlockSpec((1, 1, S, w), lambda b, h: (b, h, 0, 0))
    vt_in = lambda w: pl.BlockSpec((1, 1, nq, DV_AUG, T), lambda b, h: (b, h, 0, 0, 0))
    att_out = pl.BlockSpec((1, S, LANES), lambda b, h: (b, 0, h))
    ya = pl.pallas_call(
        functools.partial(_mla_kernel, nq=nq, T=T),
        grid=(B, H),
        in_specs=[seq_spec(MLA_QK), seq_spec(MLA_QK), vt_in(MLA_V)],
        out_specs=att_out,
        out_shape=jax.ShapeDtypeStruct((B, S, H * MLA_V), BF16),
        scratch_shapes=_flash_scratch(nq, T, DV_AUG),
        compiler_params=_params(2),
        name="mla_attn",
    )(qa, ka, va)

    yb = pl.pallas_call(
        functools.partial(_moba_kernel, nq=nq, T=T, nb=nb),
        grid=(B, H),
        in_specs=[seq_spec(MOBA_HD), seq_spec(MOBA_HD), vt_in(MOBA_HD)],
        out_specs=att_out,
        out_shape=jax.ShapeDtypeStruct((B, S, H * MOBA_HD), BF16),
        scratch_shapes=[pltpu.VMEM((S, 2 * LANES), BF16), pltpu.VMEM((S, 2 * LANES), BF16),
                        pltpu.VMEM((nb, LANES), F32)] + _flash_scratch(nq, T, DV_AUG),
        compiler_params=_params(2),
        name="moba_attn",
    )(qb, kb, vb)

    tm = TM_FFN
    n_chunks = D_FF // FF_CHUNK
    rows = lambda w: pl.BlockSpec((1, tm, w), lambda b, s: (b, s, 0))
    ffn_w = [w_o_mla.astype(BF16), w_o_moba.astype(BF16), w_out.astype(BF16),
             row2(ffn_norm), w_up.astype(BF16), conv_w, row2(conv_b), w_down.astype(BF16),
             row2(out_gain)]
    out = pl.pallas_call(
        functools.partial(_ffn_kernel, tm=tm, n_chunks=n_chunks),
        grid=(B, S // tm),
        in_specs=[rows(D), rows(D), rows(D), rows(N_BRANCH * D)]
        + [_const_spec(w.shape) for w in ffn_w],
        out_specs=pl.BlockSpec((1, tm, D), lambda b, s: (b, s, 0)),
        out_shape=jax.ShapeDtypeStruct((B, S, D), F32),
        scratch_shapes=[pltpu.VMEM((2 * n_chunks, SUBLANES, FF_CHUNK), F32),
                        pltpu.VMEM((tm, D_FF), BF16)]
        + [pltpu.VMEM((tm + SUBLANES, FF_CHUNK), F32)] * 4,
        compiler_params=_params(2),
        name="ffn",
    )(x, ya, yb, g, *ffn_w)
    return out


def kernel(x, attn_norm, w_in, b_gate, q_norm, w_uq, kv_norm, w_ukv, w_o_mla, w_o_moba,
           w_out, ffn_norm, w_up, conv_w, conv_b, w_down, final_norm):
    depth = w_in.shape[0]
    assert depth == 1
    return _layer(x, attn_norm[0], w_in[0], b_gate[0], q_norm[0], w_uq[0], kv_norm[0],
                  w_ukv[0], w_o_mla[0], w_o_moba[0], w_out[0], ffn_norm[0], w_up[0],
                  conv_w[0], conv_b[0], w_down[0], final_norm)
```

```python
import functools

import jax
import jax.numpy as jnp
from jax import lax
from jax.experimental import pallas as pl
from jax.experimental.pallas import tpu as pltpu

F32 = jnp.float32
BF16 = jnp.bfloat16

D_MODEL = 1024
MLA_HEADS = 8
MLA_NOPE = 128
MLA_ROPE = 64
MLA_V = 128
Q_LORA = 384
KV_LORA = 256
MOBA_HEADS = 8
MOBA_HD = 128
MOBA_BLOCK = 256
MOBA_TOPK = 3
D_FF = 2816
CONV_W = 3
ROPE_THETA = 10000.0
EPS = 1e-6
NEG = -1e30
N_BRANCH = 2
LOG2E = 1.4426950408889634

OFF_QLAT = 0
OFF_KVLAT = OFF_QLAT + Q_LORA
OFF_KPE = OFF_KVLAT + KV_LORA
OFF_MOBA = OFF_KPE + MLA_ROPE
OFF_GATE = OFF_MOBA + 3 * MOBA_HEADS * MOBA_HD

LANES = 128
SUBLANES = 8
BF16_ROWS = 16
MLA_QK = 2 * LANES
VMEM_LIMIT = 56 * 1024 * 1024

TM_PROJ = 512
ATT_TILE = 1024
QCOLS = 256
ITEMS_PER_STEP = 2
ATT_VMEM_LIMIT = 60 * 1024 * 1024
GATE_TILE = 2048
DV_AUG = LANES + BF16_ROWS
TM_FFN = 512
FF_CHUNK = 256


def _dot(a, b):
    return jnp.dot(a, b, preferred_element_type=F32)


def _dot_nt(a, b):
    return lax.dot_general(a, b, (((1,), (1,)), ((), ())), preferred_element_type=F32)


def _rms(x, g):
    return x * lax.rsqrt(jnp.mean(x * x, axis=-1, keepdims=True) + EPS) * g


def _const_spec(shape):
    n = len(shape)
    return pl.BlockSpec(shape, lambda *_: (0,) * n, pipeline_mode=pl.Buffered(1))


def _proj_kernel(x_ref, an_ref, wq_ref, wkv_ref, wpe_ref, wmq_ref, wmk_ref, wmv_ref, wg_ref,
                 bg_ref, qn_ref, wuqn_ref, wuqr_ref, kvn_ref, wuk_ref, wuv_ref,
                 cra_ref, sra_ref, cta_ref, sta_ref, sga_ref,
                 crb_ref, srb_ref, ctb_ref, stb_ref, sgb_ref,
                 qa_ref, ka_ref, va_ref, qb_ref, kb_ref, vb_ref, g_ref):
    hb = _rms(x_ref[0], an_ref[...]).astype(BF16)
    lane = lax.broadcasted_iota(jnp.int32, (hb.shape[0], LANES), 1)
    low_half = lane < MLA_ROPE
    first32 = (lane % MLA_ROPE) < (MLA_ROPE // 2)

    def rope_table(refs):
        cr_ref, sr_ref, ct_ref, st_ref, sign_ref = refs
        tile = pl.ds(pl.program_id(1), 1)
        cr, sr, ct, st = cr_ref[...], sr_ref[...], ct_ref[tile, :], st_ref[tile, :]
        return ct * cr - st * sr, (st * cr + ct * sr) * sign_ref[...]

    cosa, sina = rope_table((cra_ref, sra_ref, cta_ref, sta_ref, sga_ref))
    cosb, sinb = rope_table((crb_ref, srb_ref, ctb_ref, stb_ref, sgb_ref))
    ones_rows = jnp.ones((DV_AUG - LANES, hb.shape[0]), BF16)

    def rope64(c):
        partner = jnp.where(first32, pltpu.roll(c, LANES - 32, 1), pltpu.roll(c, 32, 1))
        return c * cosa + partner * sina

    def rope128(c):
        return c * cosb + pltpu.roll(c, LANES // 2, 1) * sinb

    ql = _rms(_dot(hb, wq_ref[...]), qn_ref[...]).astype(BF16)
    scale_a = (MLA_NOPE + MLA_ROPE) ** -0.5 * LOG2E
    qn = _dot(ql, wuqn_ref[...]) * scale_a
    qr = _dot(ql, wuqr_ref[...]) * scale_a
    for h in range(MLA_HEADS):
        qa_ref[0, h, :, 0:LANES] = qn[:, h * LANES:(h + 1) * LANES].astype(BF16)
    for c in range(MLA_HEADS // 2):
        r = rope64(qr[:, c * LANES:(c + 1) * LANES])
        qa_ref[0, 2 * c, :, LANES:] = jnp.where(low_half, r, 0.0).astype(BF16)
        qa_ref[0, 2 * c + 1, :, LANES:] = jnp.where(
            low_half, pltpu.roll(r, LANES // 2, 1), 0.0).astype(BF16)

    kvl = _rms(_dot(hb, wkv_ref[...]), kvn_ref[...]).astype(BF16)
    kn = _dot(kvl, wuk_ref[...])
    vt = _dot_nt(wuv_ref[...], kvl)
    krot = rope64(_dot(hb, wpe_ref[...])).astype(BF16)
    for h in range(MLA_HEADS):
        ka_ref[0, h, :, 0:LANES] = kn[:, h * LANES:(h + 1) * LANES].astype(BF16)
        ka_ref[0, h, :, LANES:] = krot
        va_ref[0, h, 0, 0:LANES] = vt[h * LANES:(h + 1) * LANES, :].astype(BF16)
        va_ref[0, h, 0, LANES:] = ones_rows

    mq = _dot(hb, wmq_ref[...])
    for h in range(MOBA_HEADS):
        qb_ref[0, h] = rope128(mq[:, h * LANES:(h + 1) * LANES])
    mk = _dot(hb, wmk_ref[...])
    for h in range(MOBA_HEADS):
        kb_ref[0, h] = rope128(mk[:, h * LANES:(h + 1) * LANES])
    mvt = _dot_nt(wmv_ref[...], hb)
    for h in range(MOBA_HEADS):
        vb_ref[0, h, 0, 0:LANES] = mvt[h * LANES:(h + 1) * LANES, :].astype(BF16)
        vb_ref[0, h, 0, LANES:] = ones_rows

    z = _dot(hb, wg_ref[...]) + bg_ref[...]
    g_ref[0] = (1.0 / (1.0 + jnp.exp(-z))).astype(BF16)


def _flash_causal(nq, T, q_rows, k_tile, vt_tile, o_ref, scr):
    ns = 2 * ITEMS_PER_STEP
    s_refs, x_refs, p_refs, a_refs = (scr[k * ns:(k + 1) * ns] for k in range(4))
    m_all, acc_all = scr[4 * ns:4 * ns + 2]

    groups = range(T // QCOLS)
    cols = lambda c: slice(c * QCOLS, (c + 1) * QCOLS)

    def keys(diag, c):
        return (c + 1) * QCOLS if diag else T

    def scores(i, kt, slot, diag, c):
        n = keys(diag, c)
        s = _dot_nt(k_tile(kt, n), q_rows(i * T + c * QCOLS))
        if diag:
            key = lax.broadcasted_iota(jnp.int32, (n, QCOLS), 0)
            qry = lax.broadcasted_iota(jnp.int32, (n, QCOLS), 1) + c * QCOLS
            s = jnp.where(key <= qry, s, NEG)
        s_refs[slot][0:n, cols(c)] = s
        x_refs[slot][:, cols(c)] = jnp.broadcast_to(jnp.max(s, axis=0, keepdims=True),
                                                    (SUBLANES, QCOLS))

    def softmax(i, slot, diag, c):
        if diag:
            m_new = x_refs[slot][:, cols(c)]
        else:
            m_old = m_all[i, :, cols(c)]
            m_new = jnp.maximum(m_old, x_refs[slot][:, cols(c)])
            alpha = jnp.exp2(m_old - m_new)
            a_refs[slot][:, cols(c)] = alpha
        for r in range(keys(diag, c) // BF16_ROWS):
            rows = slice(r * BF16_ROWS, (r + 1) * BF16_ROWS)
            p = jnp.exp2(s_refs[slot][rows, cols(c)] - m_new[0:1, :])
            p_refs[slot][rows, cols(c)] = p.astype(BF16)
        m_all[i, :, cols(c)] = m_new

    def pv(i, kt, slot, diag, c):
        n = keys(diag, c)
        r = _dot(vt_tile(kt, n), p_refs[slot][0:n, cols(c)])
        acc_all[i, :, cols(c)] = (r if diag else
                                  a_refs[slot][0:1, cols(c)] * acc_all[i, :, cols(c)] + r)

    W = ITEMS_PER_STEP
    lo, hi = tuple(range(W)), tuple(range(W, 2 * W))

    def run_pass(n_items, coord, diag):
        assert n_items % W == 0 and n_items // W >= 3
        n_steps = n_items // W

        def block(j, do_s, do_x, do_v, xs, os):
            for c in groups:
                for w in range(W):
                    if do_s:
                        scores(*coord(W * (j + 1) + w), os[w], diag, c)
                for w in range(W):
                    if do_x:
                        softmax(coord(W * j + w)[0], xs[w], diag, c)
                for w in range(W):
                    if do_v:
                        pv(*coord(W * (j - 1) + w), os[w], diag, c)

        def body(j, carry):
            lax.cond(j % 2 == 0, lambda: block(j, True, True, True, lo, hi),
                     lambda: block(j, True, True, True, hi, lo))
            return carry

        slots = lambda j: (lo, hi) if j % 2 == 0 else (hi, lo)
        block(-1, True, False, False, *slots(-1))
        block(0, True, True, False, *slots(0))
        lax.fori_loop(1, n_steps - 1, body, 0)
        block(n_steps - 1, False, True, True, *slots(n_steps - 1))
        block(n_steps, False, False, True, *slots(n_steps))

    run_pass(nq, lambda n: (n, n), True)

    def full_coord(n):
        i = 1 + sum((n >= t * (t - 1) // 2) * 1 for t in range(2, nq))
        return i, n - (i * (i - 1)) // 2

    run_pass(nq * (nq - 1) // 2, full_coord, False)

    def finish(i, c):
        acc = acc_all[i]
        o = acc[0:LANES] / acc[LANES:LANES + 1]
        o_ref[0, pl.ds(pl.multiple_of(i * T, T), T), :] = o.T.astype(BF16)
        return c

    lax.fori_loop(0, nq, finish, 0)


def _flash_scratch(nq, T, dv):
    ns = 2 * ITEMS_PER_STEP
    return ([pltpu.VMEM((T, T), F32)] * ns + [pltpu.VMEM((SUBLANES, T), F32)] * ns
            + [pltpu.VMEM((T, T), BF16)] * ns + [pltpu.VMEM((SUBLANES, T), F32)] * ns
            + [pltpu.VMEM((nq, SUBLANES, T), F32), pltpu.VMEM((nq, dv, T), F32)])


def _q_rows(ref):
    return lambda start: ref[pl.ds(pl.multiple_of(start, QCOLS), QCOLS), :]


def _k_tile(ref, T):
    return lambda kt, n: ref[pl.ds(pl.multiple_of(kt * T, T), n), :]


def _vt_tile(ref):
    return lambda kt, n: ref[0, 0, kt, :, 0:n]


def _mla_kernel(q_ref, k_ref, vt_ref, o_ref, *scr, nq, T):
    _flash_causal(nq, T, _q_rows(q_ref.at[0, 0]), _k_tile(k_ref.at[0, 0], T),
                  _vt_tile(vt_ref), o_ref, scr)


def _moba_kernel(q_ref, k_ref, vt_ref, o_ref, kaug_ref, qaug_ref, kmean_ref, *scr, nq, T, nb):
    lane = lax.broadcasted_iota(jnp.int32, (MOBA_BLOCK, LANES), 1)

    def blk_body(b, c):
        off = pl.multiple_of(b * MOBA_BLOCK, MOBA_BLOCK)
        kb = k_ref[0, 0, pl.ds(off, MOBA_BLOCK), :]
        kmean_ref[pl.ds(b, 1), :] = jnp.mean(kb, axis=0, keepdims=True)
        kaug_ref[pl.ds(off, MOBA_BLOCK), 0:LANES] = kb.astype(BF16)
        kaug_ref[pl.ds(off, MOBA_BLOCK), LANES:] = jnp.where(lane == b, 1.0, 0.0).astype(BF16)
        return c

    lax.fori_loop(0, nb, blk_body, 0)

    gt_n = GATE_TILE
    kblk = lax.broadcasted_iota(jnp.int32, (nb, gt_n), 0)
    kblk_f = kblk.astype(F32)
    qoff = lax.broadcasted_iota(jnp.int32, (nb, gt_n), 1) // MOBA_BLOCK

    def gate_body(t, c):
        off = pl.multiple_of(t * gt_n, gt_n)
        q = q_ref[0, 0, pl.ds(off, gt_n), :]
        gt = lax.dot_general(kmean_ref[...], q, (((1,), (1,)), ((), ())),
                             precision=lax.Precision.HIGHEST, preferred_element_type=F32)
        qblk = t * (gt_n // MOBA_BLOCK) + qoff
        past = kblk < qblk
        g = jnp.where(past, gt, -jnp.inf)
        sel = kblk == qblk
        for _ in range(MOBA_TOPK):
            first = jnp.min(jnp.where(g == jnp.max(g, axis=0, keepdims=True), kblk_f, float(nb)),
                            axis=0, keepdims=True)
            hit = kblk_f == first
            sel = sel | (hit & past)
            g = jnp.where(hit, -jnp.inf, g)
        keep = sel
        bias_t = jnp.where(keep, 0.0, NEG)
        bias_t = jnp.concatenate([bias_t, jnp.zeros((LANES - nb, gt_n), F32)], axis=0)
        qaug_ref[pl.ds(off, gt_n), 0:LANES] = (q * (MOBA_HD ** -0.5 * LOG2E)).astype(BF16)
        qaug_ref[pl.ds(off, gt_n), LANES:] = bias_t.T.astype(BF16)
        return c

    lax.fori_loop(0, nq * T // gt_n, gate_body, 0)

    _flash_causal(nq, T, _q_rows(qaug_ref), _k_tile(kaug_ref, T), _vt_tile(vt_ref), o_ref, scr)


def _ffn_kernel(x_ref, ya_ref, yb_ref, g_ref, woa_ref, wob_ref, wout_ref,
                fn_ref, wup_ref, cw_ref, cb_ref, wdn_ref, gn_ref, o_ref, carry_ref,
                act_ref, *bufs, tm, n_chunks):
    si = pl.program_id(1)

    @pl.when(si == 0)
    def _():
        carry_ref[...] = jnp.zeros_like(carry_ref)

    g = g_ref[0].astype(F32)
    mix = (g[:, :D_MODEL] * _dot(ya_ref[0], woa_ref[...])
           + g[:, D_MODEL:] * _dot(yb_ref[0], wob_ref[...]))
    x = x_ref[0] + _dot(mix.astype(BF16), wout_ref[...])
    hb = _rms(x, fn_ref[...]).astype(BF16)

    def conv(u, buf, idx, col0):
        buf[0:SUBLANES, :] = carry_ref[idx]
        buf[SUBLANES:, :] = u
        carry_ref[idx] = u[tm - SUBLANES:, :]
        u1 = buf[SUBLANES - 1:SUBLANES - 1 + tm, :]
        u2 = buf[SUBLANES - 2:SUBLANES - 2 + tm, :]
        w = cw_ref[:, col0:col0 + FF_CHUNK]
        return cb_ref[:, col0:col0 + FF_CHUNK] + w[0:1] * u2 + w[1:2] * u1 + w[2:3] * u

    for c in range(n_chunks):
        ca = c * FF_CHUNK
        cb = D_FF + c * FF_CHUNK
        ya = conv(_dot(hb, wup_ref[:, ca:ca + FF_CHUNK]), bufs[2 * (c % 2)], c, ca)
        yb = conv(_dot(hb, wup_ref[:, cb:cb + FF_CHUNK]), bufs[2 * (c % 2) + 1], n_chunks + c, cb)
        act_ref[:, ca:ca + FF_CHUNK] = (ya * (1.0 / (1.0 + jnp.exp(-ya))) * yb).astype(BF16)
    o_ref[0] = _rms(x + _dot(act_ref[...], wdn_ref[...]), gn_ref[...])


def _rope_tables(seq, d, tm):
    inv = ROPE_THETA ** (-jnp.arange(0, d, 2, dtype=F32) / d)
    lanes = lambda t: jnp.tile(jnp.concatenate([t, t], axis=-1), (1, LANES // d))
    row_ang = jnp.arange(tm).astype(F32)[:, None] * inv[None, :]
    tile_ang = (jnp.arange(seq // tm) * tm).astype(F32)[:, None] * inv[None, :]
    half = jnp.ones((1, d // 2), F32)
    sign = jnp.tile(jnp.concatenate([-half, half], axis=-1), (1, LANES // d))
    return [lanes(jnp.cos(row_ang)), lanes(jnp.sin(row_ang)),
            lanes(jnp.cos(tile_ang)), lanes(jnp.sin(tile_ang)), sign]


def _params(n_axes, vmem_limit=VMEM_LIMIT):
    return pltpu.CompilerParams(dimension_semantics=("arbitrary",) * n_axes,
                                vmem_limit_bytes=vmem_limit)


def _layer(x, attn_norm, w_in, b_gate, q_norm, w_uq, kv_norm, w_ukv, w_o_mla, w_o_moba,
           w_out, ffn_norm, w_up, conv_w, conv_b, w_down, out_gain):
    B, S, D = x.shape
    H = MLA_HEADS
    T = ATT_TILE
    assert D == D_MODEL and S % T == 0 and S // T >= 3 and S % GATE_TILE == 0
    assert D_FF % FF_CHUNK == 0 and GATE_TILE % MOBA_BLOCK == 0 and T % TM_PROJ == 0
    nq = S // T
    nb = S // MOBA_BLOCK
    assert nb <= LANES and nb % SUBLANES == 0

    wcols = lambda a, b: w_in[:, a:b].astype(BF16)
    wq = wcols(OFF_QLAT, OFF_KVLAT)
    wkv = wcols(OFF_KVLAT, OFF_KPE)
    wpe = jnp.pad(wcols(OFF_KPE, OFF_MOBA), ((0, 0), (0, LANES - MLA_ROPE)))
    hd = MOBA_HEADS * MOBA_HD
    wmq = wcols(OFF_MOBA, OFF_MOBA + hd)
    wmk = wcols(OFF_MOBA + hd, OFF_MOBA + 2 * hd)
    wmv = wcols(OFF_MOBA + 2 * hd, OFF_GATE).T
    wg = wcols(OFF_GATE, OFF_GATE + N_BRANCH * D_MODEL)
    wuq3 = w_uq.astype(BF16).reshape(Q_LORA, H, MLA_NOPE + MLA_ROPE)
    wuqn = wuq3[:, :, :MLA_NOPE].reshape(Q_LORA, H * MLA_NOPE)
    wuqr = wuq3[:, :, MLA_NOPE:].reshape(Q_LORA, H * MLA_ROPE)
    wukv3 = w_ukv.astype(BF16).reshape(KV_LORA, H, MLA_NOPE + MLA_V)
    wuk = wukv3[:, :, :MLA_NOPE].reshape(KV_LORA, H * MLA_NOPE)
    wuv = wukv3[:, :, MLA_NOPE:].reshape(KV_LORA, H * MLA_V).T
    row2 = lambda v: v.reshape(1, -1)
    tm = TM_PROJ
    ropes = _rope_tables(S, MLA_ROPE, tm) + _rope_tables(S, MOBA_HD, tm)
    head_out = lambda w, dt: jax.ShapeDtypeStruct((B, H, S, w), dt)
    head_spec = lambda w: pl.BlockSpec((1, H, tm, w), lambda b, s: (b, 0, s, 0))
    sub = T // tm
    assert MLA_V == LANES and MOBA_HD == LANES
    vt_out = lambda w: jax.ShapeDtypeStruct((B, H, nq, DV_AUG, T), BF16)
    vt_spec = lambda w: pl.BlockSpec((1, H, 1, DV_AUG, tm),
                                     lambda b, s: (b, 0, s // sub, 0, s % sub))
    weights = [row2(attn_norm), wq, wkv, wpe, wmq, wmk, wmv, wg, row2(b_gate), row2(q_norm),
               wuqn, wuqr, row2(kv_norm), wuk, wuv]
    qa, ka, va, qb, kb, vb, g = pl.pallas_call(
        _proj_kernel,
        grid=(B, S // tm),
        in_specs=[pl.BlockSpec((1, tm, D), lambda b, s: (b, s, 0))]
        + [_const_spec(w.shape) for w in weights + ropes],
        out_specs=[head_spec(MLA_QK), head_spec(MLA_QK), vt_spec(MLA_V),
                   head_spec(MOBA_HD), head_spec(MOBA_HD), vt_spec(MOBA_HD),
                   pl.BlockSpec((1, tm, N_BRANCH * D), lambda b, s: (b, s, 0))],
        out_shape=[head_out(MLA_QK, BF16), head_out(MLA_QK, BF16), vt_out(MLA_V),
                   head_out(MOBA_HD, F32), head_out(MOBA_HD, F32), vt_out(MOBA_HD),
                   jax.ShapeDtypeStruct((B, S, N_BRANCH * D), BF16)],
        compiler_params=_params(2),
        name="proj",
    )(x, *weights, *ropes)

    seq_spec = lambda w, **kw: pl.BlockSpec((1, 1, S, w), lambda b, h: (b, h, 0, 0), **kw)
    vt_in = lambda w: pl.BlockSpec((1, 1, nq, DV_AUG, T), lambda b, h: (b, h, 0, 0, 0))
    att_out = pl.BlockSpec((1, S, LANES), lambda b, h: (b, 0, h))
    ya = pl.pallas_call(
        functools.partial(_mla_kernel, nq=nq, T=T),
        grid=(B, H),
        in_specs=[seq_spec(MLA_QK), seq_spec(MLA_QK), vt_in(MLA_V)],
        out_specs=att_out,
        out_shape=jax.ShapeDtypeStruct((B, S, H * MLA_V), BF16),
        scratch_shapes=_flash_scratch(nq, T, DV_AUG),
        compiler_params=_params(2, ATT_VMEM_LIMIT),
        name="mla_attn",
    )(qa, ka, va)

    once = dict(pipeline_mode=pl.Buffered(1))
    yb = pl.pallas_call(
        functools.partial(_moba_kernel, nq=nq, T=T, nb=nb),
        grid=(B, H),
        in_specs=[seq_spec(MOBA_HD, **once), seq_spec(MOBA_HD, **once), vt_in(MOBA_HD)],
        out_specs=att_out,
        out_shape=jax.ShapeDtypeStruct((B, S, H * MOBA_HD), BF16),
        scratch_shapes=[pltpu.VMEM((S, 2 * LANES), BF16), pltpu.VMEM((S, 2 * LANES), BF16),
                        pltpu.VMEM((nb, LANES), F32)] + _flash_scratch(nq, T, DV_AUG),
        compiler_params=_params(2, ATT_VMEM_LIMIT),
        name="moba_attn",
    )(qb, kb, vb)

    tm = TM_FFN
    n_chunks = D_FF // FF_CHUNK
    rows = lambda w: pl.BlockSpec((1, tm, w), lambda b, s: (b, s, 0))
    ffn_w = [w_o_mla.astype(BF16), w_o_moba.astype(BF16), w_out.astype(BF16),
             row2(ffn_norm), w_up.astype(BF16), conv_w, row2(conv_b), w_down.astype(BF16),
             row2(out_gain)]
    out = pl.pallas_call(
        functools.partial(_ffn_kernel, tm=tm, n_chunks=n_chunks),
        grid=(B, S // tm),
        in_specs=[rows(D), rows(D), rows(D), rows(N_BRANCH * D)]
        + [_const_spec(w.shape) for w in ffn_w],
        out_specs=pl.BlockSpec((1, tm, D), lambda b, s: (b, s, 0)),
        out_shape=jax.ShapeDtypeStruct((B, S, D), F32),
        scratch_shapes=[pltpu.VMEM((2 * n_chunks, SUBLANES, FF_CHUNK), F32),
                        pltpu.VMEM((tm, D_FF), BF16)]
        + [pltpu.VMEM((tm + SUBLANES, FF_CHUNK), F32)] * 4,
        compiler_params=_params(2),
        name="ffn",
    )(x, ya, yb, g, *ffn_w)
    return out


def kernel(x, attn_norm, w_in, b_gate, q_norm, w_uq, kv_norm, w_ukv, w_o_mla, w_o_moba,
           w_out, ffn_norm, w_up, conv_w, conv_b, w_down, final_norm):
    depth = w_in.shape[0]
    assert depth == 1
    return _layer(x, attn_norm[0], w_in[0], b_gate[0], q_norm[0], w_uq[0], kv_norm[0],
                  w_ukv[0], w_o_mla[0], w_o_moba[0], w_out[0], ffn_norm[0], w_up[0],
                  conv_w[0], conv_b[0], w_down[0], final_norm)
```

```python
import functools

import jax
import jax.numpy as jnp
from jax import lax
from jax.experimental import pallas as pl
from jax.experimental.pallas import tpu as pltpu

F32 = jnp.float32
BF16 = jnp.bfloat16

D_MODEL = 1024
MLA_HEADS = 8
MLA_NOPE = 128
MLA_ROPE = 64
MLA_V = 128
Q_LORA = 384
KV_LORA = 256
MOBA_HEADS = 8
MOBA_HD = 128
MOBA_BLOCK = 256
MOBA_TOPK = 3
D_FF = 2816
CONV_W = 3
ROPE_THETA = 10000.0
EPS = 1e-6
NEG = -1e30
N_BRANCH = 2
LOG2E = 1.4426950408889634

OFF_QLAT = 0
OFF_KVLAT = OFF_QLAT + Q_LORA
OFF_KPE = OFF_KVLAT + KV_LORA
OFF_MOBA = OFF_KPE + MLA_ROPE
OFF_GATE = OFF_MOBA + 3 * MOBA_HEADS * MOBA_HD

LANES = 128
SUBLANES = 8
BF16_ROWS = 16
MLA_QK = 2 * LANES
VMEM_LIMIT = 56 * 1024 * 1024

TM_PROJ = 512
ATT_TILE = 1024
QCOLS = 256
ITEMS_PER_STEP = 1
SCORE_GROUPS = 2
ATT_VMEM_LIMIT = 60 * 1024 * 1024
GATE_TILE = 2048
DV_AUG = LANES + BF16_ROWS
TM_FFN = 512
FF_CHUNK = 256


def _dot(a, b):
    return jnp.dot(a, b, preferred_element_type=F32)


def _dot_nt(a, b):
    return lax.dot_general(a, b, (((1,), (1,)), ((), ())), preferred_element_type=F32)


def _rms(x, g):
    return x * lax.rsqrt(jnp.mean(x * x, axis=-1, keepdims=True) + EPS) * g


def _const_spec(shape):
    n = len(shape)
    return pl.BlockSpec(shape, lambda *_: (0,) * n, pipeline_mode=pl.Buffered(1))


def _proj_kernel(x_ref, an_ref, wq_ref, wkv_ref, wpe_ref, wmq_ref, wmk_ref, wmv_ref, wg_ref,
                 bg_ref, qn_ref, wuqn_ref, wuqr_ref, kvn_ref, wuk_ref, wuv_ref,
                 cra_ref, sra_ref, cta_ref, sta_ref, sga_ref,
                 crb_ref, srb_ref, ctb_ref, stb_ref, sgb_ref,
                 qa_ref, ka_ref, va_ref, qb_ref, kb_ref, vb_ref, g_ref):
    hb = _rms(x_ref[0], an_ref[...]).astype(BF16)
    lane = lax.broadcasted_iota(jnp.int32, (hb.shape[0], LANES), 1)
    low_half = lane < MLA_ROPE
    first32 = (lane % MLA_ROPE) < (MLA_ROPE // 2)

    def rope_table(refs):
        cr_ref, sr_ref, ct_ref, st_ref, sign_ref = refs
        tile = pl.ds(pl.program_id(1), 1)
        cr, sr, ct, st = cr_ref[...], sr_ref[...], ct_ref[tile, :], st_ref[tile, :]
        return ct * cr - st * sr, (st * cr + ct * sr) * sign_ref[...]

    cosa, sina = rope_table((cra_ref, sra_ref, cta_ref, sta_ref, sga_ref))
    cosb, sinb = rope_table((crb_ref, srb_ref, ctb_ref, stb_ref, sgb_ref))
    ones_rows = jnp.ones((DV_AUG - LANES, hb.shape[0]), BF16)

    def rope64(c):
        partner = jnp.where(first32, pltpu.roll(c, LANES - 32, 1), pltpu.roll(c, 32, 1))
        return c * cosa + partner * sina

    def rope128(c):
        return c * cosb + pltpu.roll(c, LANES // 2, 1) * sinb

    ql = _rms(_dot(hb, wq_ref[...]), qn_ref[...]).astype(BF16)
    scale_a = (MLA_NOPE + MLA_ROPE) ** -0.5 * LOG2E
    qn = _dot(ql, wuqn_ref[...]) * scale_a
    qr = _dot(ql, wuqr_ref[...]) * scale_a
    for h in range(MLA_HEADS):
        qa_ref[0, h, :, 0:LANES] = qn[:, h * LANES:(h + 1) * LANES].astype(BF16)
    for c in range(MLA_HEADS // 2):
        r = rope64(qr[:, c * LANES:(c + 1) * LANES])
        qa_ref[0, 2 * c, :, LANES:] = jnp.where(low_half, r, 0.0).astype(BF16)
        qa_ref[0, 2 * c + 1, :, LANES:] = jnp.where(
            low_half, pltpu.roll(r, LANES // 2, 1), 0.0).astype(BF16)

    kvl = _rms(_dot(hb, wkv_ref[...]), kvn_ref[...]).astype(BF16)
    kn = _dot(kvl, wuk_ref[...])
    vt = _dot_nt(wuv_ref[...], kvl)
    krot = rope64(_dot(hb, wpe_ref[...])).astype(BF16)
    for h in range(MLA_HEADS):
        ka_ref[0, h, :, 0:LANES] = kn[:, h * LANES:(h + 1) * LANES].astype(BF16)
        ka_ref[0, h, :, LANES:] = krot
        va_ref[0, h, 0, 0:LANES] = vt[h * LANES:(h + 1) * LANES, :].astype(BF16)
        va_ref[0, h, 0, LANES:] = ones_rows

    mq = _dot(hb, wmq_ref[...])
    for h in range(MOBA_HEADS):
        qb_ref[0, h] = rope128(mq[:, h * LANES:(h + 1) * LANES])
    mk = _dot(hb, wmk_ref[...])
    for h in range(MOBA_HEADS):
        kb_ref[0, h] = rope128(mk[:, h * LANES:(h + 1) * LANES])
    mvt = _dot_nt(wmv_ref[...], hb)
    for h in range(MOBA_HEADS):
        vb_ref[0, h, 0, 0:LANES] = mvt[h * LANES:(h + 1) * LANES, :].astype(BF16)
        vb_ref[0, h, 0, LANES:] = ones_rows

    z = _dot(hb, wg_ref[...]) + bg_ref[...]
    g_ref[0] = (1.0 / (1.0 + jnp.exp(-z))).astype(BF16)


def _flash_causal(nq, T, q_rows, k_tile, vt_tile, o_ref, scr):
    ns = 2 * ITEMS_PER_STEP
    s_refs, x_refs, p_refs, a_refs = (scr[k * ns:(k + 1) * ns] for k in range(4))
    m_all, acc_all = scr[4 * ns:4 * ns + 2]

    groups = range(T // QCOLS)
    cols = lambda c: slice(c * QCOLS, (c + 1) * QCOLS)

    def keys(diag, c):
        return (c + 1) * QCOLS if diag else T

    def scores(i, kt, slot, diag, c, ng):
        n, width = keys(diag, c), ng * QCOLS
        span = slice(c * QCOLS, c * QCOLS + width)
        s = _dot_nt(k_tile(kt, n), q_rows(i * T + c * QCOLS, width))
        if diag:
            key = lax.broadcasted_iota(jnp.int32, (n, width), 0)
            qry = lax.broadcasted_iota(jnp.int32, (n, width), 1) + c * QCOLS
            s = jnp.where(key <= qry, s, NEG)
        s_refs[slot][0:n, span] = s
        x_refs[slot][:, span] = jnp.broadcast_to(jnp.max(s, axis=0, keepdims=True),
                                                 (SUBLANES, width))

    def softmax(i, slot, diag, c):
        if diag:
            m_new = x_refs[slot][:, cols(c)]
        else:
            m_old = m_all[i, :, cols(c)]
            m_new = jnp.maximum(m_old, x_refs[slot][:, cols(c)])
            alpha = jnp.exp2(m_old - m_new)
            a_refs[slot][:, cols(c)] = alpha
        for r in range(keys(diag, c) // BF16_ROWS):
            rows = slice(r * BF16_ROWS, (r + 1) * BF16_ROWS)
            p = jnp.exp2(s_refs[slot][rows, cols(c)] - m_new[0:1, :])
            p_refs[slot][rows, cols(c)] = p.astype(BF16)
        m_all[i, :, cols(c)] = m_new

    def pv(i, kt, slot, diag, c):
        n = keys(diag, c)
        r = _dot(vt_tile(kt, n), p_refs[slot][0:n, cols(c)])
        acc_all[i, :, cols(c)] = (r if diag else
                                  a_refs[slot][0:1, cols(c)] * acc_all[i, :, cols(c)] + r)

    W = ITEMS_PER_STEP
    lo, hi = tuple(range(W)), tuple(range(W, 2 * W))

    def run_pass(n_items, coord, diag):
        assert n_items % W == 0 and n_items // W >= 3
        n_steps = n_items // W

        ng = 1 if diag else SCORE_GROUPS

        def block(j, do_s, do_x, do_v, xs, os):
            for c in groups:
                for w in range(W):
                    if do_s and c % ng == 0:
                        scores(*coord(W * (j + 1) + w), os[w], diag, c, ng)
                for w in range(W):
                    if do_x:
                        softmax(coord(W * j + w)[0], xs[w], diag, c)
                for w in range(W):
                    if do_v:
                        pv(*coord(W * (j - 1) + w), os[w], diag, c)

        def body(j, carry):
            lax.cond(j % 2 == 0, lambda: block(j, True, True, True, lo, hi),
                     lambda: block(j, True, True, True, hi, lo))
            return carry

        slots = lambda j: (lo, hi) if j % 2 == 0 else (hi, lo)
        block(-1, True, False, False, *slots(-1))
        block(0, True, True, False, *slots(0))
        lax.fori_loop(1, n_steps - 1, body, 0)
        block(n_steps - 1, False, True, True, *slots(n_steps - 1))
        block(n_steps, False, False, True, *slots(n_steps))

    run_pass(nq, lambda n: (n, n), True)

    def full_coord(n):
        i = 1 + sum((n >= t * (t - 1) // 2) * 1 for t in range(2, nq))
        return i, n - (i * (i - 1)) // 2

    run_pass(nq * (nq - 1) // 2, full_coord, False)

    def finish(i, c):
        acc = acc_all[i]
        o = acc[0:LANES] / acc[LANES:LANES + 1]
        o_ref[0, pl.ds(pl.multiple_of(i * T, T), T), :] = o.T.astype(BF16)
        return c

    lax.fori_loop(0, nq, finish, 0)


def _flash_scratch(nq, T, dv):
    ns = 2 * ITEMS_PER_STEP
    return ([pltpu.VMEM((T, T), F32)] * ns + [pltpu.VMEM((SUBLANES, T), F32)] * ns
            + [pltpu.VMEM((T, T), BF16)] * ns + [pltpu.VMEM((SUBLANES, T), F32)] * ns
            + [pltpu.VMEM((nq, SUBLANES, T), F32), pltpu.VMEM((nq, dv, T), F32)])


def _q_rows(ref):
    return lambda start, width: ref[pl.ds(pl.multiple_of(start, QCOLS), width), :]


def _k_tile(ref, T):
    return lambda kt, n: ref[pl.ds(pl.multiple_of(kt * T, T), n), :]


def _vt_tile(ref):
    return lambda kt, n: ref[0, 0, kt, :, 0:n]


def _mla_kernel(q_ref, k_ref, vt_ref, o_ref, *scr, nq, T):
    _flash_causal(nq, T, _q_rows(q_ref.at[0, 0]), _k_tile(k_ref.at[0, 0], T),
                  _vt_tile(vt_ref), o_ref, scr)


def _moba_kernel(q_ref, k_ref, vt_ref, o_ref, kaug_ref, qaug_ref, kmean_ref, *scr, nq, T, nb):
    lane = lax.broadcasted_iota(jnp.int32, (MOBA_BLOCK, LANES), 1)

    def blk_body(b, c):
        off = pl.multiple_of(b * MOBA_BLOCK, MOBA_BLOCK)
        kb = k_ref[0, 0, pl.ds(off, MOBA_BLOCK), :]
        kmean_ref[pl.ds(b, 1), :] = jnp.mean(kb, axis=0, keepdims=True)
        kaug_ref[pl.ds(off, MOBA_BLOCK), 0:LANES] = kb.astype(BF16)
        kaug_ref[pl.ds(off, MOBA_BLOCK), LANES:] = jnp.where(lane == b, 1.0, 0.0).astype(BF16)
        return c

    lax.fori_loop(0, nb, blk_body, 0)

    gt_n = GATE_TILE
    kblk = lax.broadcasted_iota(jnp.int32, (nb, gt_n), 0)
    kblk_f = kblk.astype(F32)
    qoff = lax.broadcasted_iota(jnp.int32, (nb, gt_n), 1) // MOBA_BLOCK

    def gate_body(t, c):
        off = pl.multiple_of(t * gt_n, gt_n)
        q = q_ref[0, 0, pl.ds(off, gt_n), :]
        gt = lax.dot_general(kmean_ref[...], q, (((1,), (1,)), ((), ())),
                             precision=lax.Precision.HIGHEST, preferred_element_type=F32)
        qblk = t * (gt_n // MOBA_BLOCK) + qoff
        past = kblk < qblk
        g = jnp.where(past, gt, -jnp.inf)
        sel = kblk == qblk
        for _ in range(MOBA_TOPK):
            first = jnp.min(jnp.where(g == jnp.max(g, axis=0, keepdims=True), kblk_f, float(nb)),
                            axis=0, keepdims=True)
            hit = kblk_f == first
            sel = sel | (hit & past)
            g = jnp.where(hit, -jnp.inf, g)
        keep = sel
        bias_t = jnp.where(keep, 0.0, NEG)
        bias_t = jnp.concatenate([bias_t, jnp.zeros((LANES - nb, gt_n), F32)], axis=0)
        qaug_ref[pl.ds(off, gt_n), 0:LANES] = (q * (MOBA_HD ** -0.5 * LOG2E)).astype(BF16)
        qaug_ref[pl.ds(off, gt_n), LANES:] = bias_t.T.astype(BF16)
        return c

    lax.fori_loop(0, nq * T // gt_n, gate_body, 0)

    _flash_causal(nq, T, _q_rows(qaug_ref), _k_tile(kaug_ref, T), _vt_tile(vt_ref), o_ref, scr)


def _ffn_kernel(x_ref, ya_ref, yb_ref, g_ref, woa_ref, wob_ref, wout_ref,
                fn_ref, wup_ref, cw_ref, cb_ref, wdn_ref, gn_ref, o_ref, carry_ref,
                act_ref, *bufs, tm, n_chunks):
    si = pl.program_id(1)

    @pl.when(si == 0)
    def _():
        carry_ref[...] = jnp.zeros_like(carry_ref)

    g = g_ref[0].astype(F32)
    mix = (g[:, :D_MODEL] * _dot(ya_ref[0], woa_ref[...])
           + g[:, D_MODEL:] * _dot(yb_ref[0], wob_ref[...]))
    x = x_ref[0] + _dot(mix.astype(BF16), wout_ref[...])
    hb = _rms(x, fn_ref[...]).astype(BF16)

    def conv(u, buf, idx, col0):
        buf[0:SUBLANES, :] = carry_ref[idx]
        buf[SUBLANES:, :] = u
        carry_ref[idx] = u[tm - SUBLANES:, :]
        u1 = buf[SUBLANES - 1:SUBLANES - 1 + tm, :]
        u2 = buf[SUBLANES - 2:SUBLANES - 2 + tm, :]
        w = cw_ref[:, col0:col0 + FF_CHUNK]
        return cb_ref[:, col0:col0 + FF_CHUNK] + w[0:1] * u2 + w[1:2] * u1 + w[2:3] * u

    for c in range(n_chunks):
        ca = c * FF_CHUNK
        cb = D_FF + c * FF_CHUNK
        ya = conv(_dot(hb, wup_ref[:, ca:ca + FF_CHUNK]), bufs[2 * (c % 2)], c, ca)
        yb = conv(_dot(hb, wup_ref[:, cb:cb + FF_CHUNK]), bufs[2 * (c % 2) + 1], n_chunks + c, cb)
        act_ref[:, ca:ca + FF_CHUNK] = (ya * (1.0 / (1.0 + jnp.exp(-ya))) * yb).astype(BF16)
    o_ref[0] = _rms(x + _dot(act_ref[...], wdn_ref[...]), gn_ref[...])


def _rope_tables(seq, d, tm):
    inv = ROPE_THETA ** (-jnp.arange(0, d, 2, dtype=F32) / d)
    lanes = lambda t: jnp.tile(jnp.concatenate([t, t], axis=-1), (1, LANES // d))
    row_ang = jnp.arange(tm).astype(F32)[:, None] * inv[None, :]
    tile_ang = (jnp.arange(seq // tm) * tm).astype(F32)[:, None] * inv[None, :]
    half = jnp.ones((1, d // 2), F32)
    sign = jnp.tile(jnp.concatenate([-half, half], axis=-1), (1, LANES // d))
    return [lanes(jnp.cos(row_ang)), lanes(jnp.sin(row_ang)),
            lanes(jnp.cos(tile_ang)), lanes(jnp.sin(tile_ang)), sign]


def _params(n_axes, vmem_limit=VMEM_LIMIT):
    return pltpu.CompilerParams(dimension_semantics=("arbitrary",) * n_axes,
                                vmem_limit_bytes=vmem_limit)


def _layer(x, attn_norm, w_in, b_gate, q_norm, w_uq, kv_norm, w_ukv, w_o_mla, w_o_moba,
           w_out, ffn_norm, w_up, conv_w, conv_b, w_down, out_gain):
    B, S, D = x.shape
    H = MLA_HEADS
    T = ATT_TILE
    assert D == D_MODEL and S % T == 0 and S // T >= 3 and S % GATE_TILE == 0
    assert D_FF % FF_CHUNK == 0 and GATE_TILE % MOBA_BLOCK == 0 and T % TM_PROJ == 0
    nq = S // T
    nb = S // MOBA_BLOCK
    assert nb <= LANES and nb % SUBLANES == 0

    wcols = lambda a, b: w_in[:, a:b].astype(BF16)
    wq = wcols(OFF_QLAT, OFF_KVLAT)
    wkv = wcols(OFF_KVLAT, OFF_KPE)
    wpe = jnp.pad(wcols(OFF_KPE, OFF_MOBA), ((0, 0), (0, LANES - MLA_ROPE)))
    hd = MOBA_HEADS * MOBA_HD
    wmq = wcols(OFF_MOBA, OFF_MOBA + hd)
    wmk = wcols(OFF_MOBA + hd, OFF_MOBA + 2 * hd)
    wmv = wcols(OFF_MOBA + 2 * hd, OFF_GATE).T
    wg = wcols(OFF_GATE, OFF_GATE + N_BRANCH * D_MODEL)
    wuq3 = w_uq.astype(BF16).reshape(Q_LORA, H, MLA_NOPE + MLA_ROPE)
    wuqn = wuq3[:, :, :MLA_NOPE].reshape(Q_LORA, H * MLA_NOPE)
    wuqr = wuq3[:, :, MLA_NOPE:].reshape(Q_LORA, H * MLA_ROPE)
    wukv3 = w_ukv.astype(BF16).reshape(KV_LORA, H, MLA_NOPE + MLA_V)
    wuk = wukv3[:, :, :MLA_NOPE].reshape(KV_LORA, H * MLA_NOPE)
    wuv = wukv3[:, :, MLA_NOPE:].reshape(KV_LORA, H * MLA_V).T
    row2 = lambda v: v.reshape(1, -1)
    tm = TM_PROJ
    ropes = _rope_tables(S, MLA_ROPE, tm) + _rope_tables(S, MOBA_HD, tm)
    head_out = lambda w, dt: jax.ShapeDtypeStruct((B, H, S, w), dt)
    head_spec = lambda w: pl.BlockSpec((1, H, tm, w), lambda b, s: (b, 0, s, 0))
    sub = T // tm
    assert MLA_V == LANES and MOBA_HD == LANES
    vt_out = lambda w: jax.ShapeDtypeStruct((B, H, nq, DV_AUG, T), BF16)
    vt_spec = lambda w: pl.BlockSpec((1, H, 1, DV_AUG, tm),
                                     lambda b, s: (b, 0, s // sub, 0, s % sub))
    weights = [row2(attn_norm), wq, wkv, wpe, wmq, wmk, wmv, wg, row2(b_gate), row2(q_norm),
               wuqn, wuqr, row2(kv_norm), wuk, wuv]
    qa, ka, va, qb, kb, vb, g = pl.pallas_call(
        _proj_kernel,
        grid=(B, S // tm),
        in_specs=[pl.BlockSpec((1, tm, D), lambda b, s: (b, s, 0))]
        + [_const_spec(w.shape) for w in weights + ropes],
        out_specs=[head_spec(MLA_QK), head_spec(MLA_QK), vt_spec(MLA_V),
                   head_spec(MOBA_HD), head_spec(MOBA_HD), vt_spec(MOBA_HD),
                   pl.BlockSpec((1, tm, N_BRANCH * D), lambda b, s: (b, s, 0))],
        out_shape=[head_out(MLA_QK, BF16), head_out(MLA_QK, BF16), vt_out(MLA_V),
                   head_out(MOBA_HD, F32), head_out(MOBA_HD, F32), vt_out(MOBA_HD),
                   jax.ShapeDtypeStruct((B, S, N_BRANCH * D), BF16)],
        compiler_params=_params(2),
        name="proj",
    )(x, *weights, *ropes)

    seq_spec = lambda w, **kw: pl.BlockSpec((1, 1, S, w), lambda b, h: (b, h, 0, 0), **kw)
    vt_in = lambda w: pl.BlockSpec((1, 1, nq, DV_AUG, T), lambda b, h: (b, h, 0, 0, 0))
    att_out = pl.BlockSpec((1, S, LANES), lambda b, h: (b, 0, h))
    ya = pl.pallas_call(
        functools.partial(_mla_kernel, nq=nq, T=T),
        grid=(B, H),
        in_specs=[seq_spec(MLA_QK), seq_spec(MLA_QK), vt_in(MLA_V)],
        out_specs=att_out,
        out_shape=jax.ShapeDtypeStruct((B, S, H * MLA_V), BF16),
        scratch_shapes=_flash_scratch(nq, T, DV_AUG),
        compiler_params=_params(2, ATT_VMEM_LIMIT),
        name="mla_attn",
    )(qa, ka, va)

    yb = pl.pallas_call(
        functools.partial(_moba_kernel, nq=nq, T=T, nb=nb),
        grid=(B, H),
        in_specs=[seq_spec(MOBA_HD), seq_spec(MOBA_HD), vt_in(MOBA_HD)],
        out_specs=att_out,
        out_shape=jax.ShapeDtypeStruct((B, S, H * MOBA_HD), BF16),
        scratch_shapes=[pltpu.VMEM((S, 2 * LANES), BF16), pltpu.VMEM((S, 2 * LANES), BF16),
                        pltpu.VMEM((nb, LANES), F32)] + _flash_scratch(nq, T, DV_AUG),
        compiler_params=_params(2, ATT_VMEM_LIMIT),
        name="moba_attn",
    )(qb, kb, vb)

    tm = TM_FFN
    n_chunks = D_FF // FF_CHUNK
    rows = lambda w: pl.BlockSpec((1, tm, w), lambda b, s: (b, s, 0))
    ffn_w = [w_o_mla.astype(BF16), w_o_moba.astype(BF16), w_out.astype(BF16),
             row2(ffn_norm), w_up.astype(BF16), conv_w, row2(conv_b), w_down.astype(BF16),
             row2(out_gain)]
    out = pl.pallas_call(
        functools.partial(_ffn_kernel, tm=tm, n_chunks=n_chunks),
        grid=(B, S // tm),
        in_specs=[rows(D), rows(D), rows(D), rows(N_BRANCH * D)]
        + [_const_spec(w.shape) for w in ffn_w],
        out_specs=pl.BlockSpec((1, tm, D), lambda b, s: (b, s, 0)),
        out_shape=jax.ShapeDtypeStruct((B, S, D), F32),
        scratch_shapes=[pltpu.VMEM((2 * n_chunks, SUBLANES, FF_CHUNK), F32),
                        pltpu.VMEM((tm, D_FF), BF16)]
        + [pltpu.VMEM((tm + SUBLANES, FF_CHUNK), F32)] * 4,
        compiler_params=_params(2),
        name="ffn",
    )(x, ya, yb, g, *ffn_w)
    return out


def kernel(x, attn_norm, w_in, b_gate, q_norm, w_uq, kv_norm, w_ukv, w_o_mla, w_o_moba,
           w_out, ffn_norm, w_up, conv_w, conv_b, w_down, final_norm):
    depth = w_in.shape[0]
    assert depth == 1
    return _layer(x, attn_norm[0], w_in[0], b_gate[0], q_norm[0], w_uq[0], kv_norm[0],
                  w_ukv[0], w_o_mla[0], w_o_moba[0], w_out[0], ffn_norm[0], w_up[0],
                  conv_w[0], conv_b[0], w_down[0], final_norm)
```

```python
import functools

import jax
import jax.numpy as jnp
from jax import lax
from jax.experimental import pallas as pl
from jax.experimental.pallas import tpu as pltpu

F32 = jnp.float32
BF16 = jnp.bfloat16

D_MODEL = 1024
MLA_HEADS = 8
MLA_NOPE = 128
MLA_ROPE = 64
MLA_V = 128
Q_LORA = 384
KV_LORA = 256
MOBA_HEADS = 8
MOBA_HD = 128
MOBA_BLOCK = 256
MOBA_TOPK = 3
D_FF = 2816
CONV_W = 3
ROPE_THETA = 10000.0
EPS = 1e-6
NEG = -1e30
N_BRANCH = 2
LOG2E = 1.4426950408889634

OFF_QLAT = 0
OFF_KVLAT = OFF_QLAT + Q_LORA
OFF_KPE = OFF_KVLAT + KV_LORA
OFF_MOBA = OFF_KPE + MLA_ROPE
OFF_GATE = OFF_MOBA + 3 * MOBA_HEADS * MOBA_HD

LANES = 128
SUBLANES = 8
BF16_ROWS = 16
ATT_QK = 2 * LANES
VMEM_LIMIT = 56 * 1024 * 1024

TM_PROJ = 512
ATT_TILE = 1024
QCOLS = 256
ITEMS_PER_STEP = 1
SCORE_GROUPS = 2
ATT_VMEM_LIMIT = 60 * 1024 * 1024
DV_AUG = LANES + BF16_ROWS
TM_FFN = 512
FF_CHUNK = 256


def _dot(a, b):
    return jnp.dot(a, b, preferred_element_type=F32)


def _dot_nt(a, b):
    return lax.dot_general(a, b, (((1,), (1,)), ((), ())), preferred_element_type=F32)


def _rms(x, g):
    return x * lax.rsqrt(jnp.mean(x * x, axis=-1, keepdims=True) + EPS) * g


def _const_spec(shape):
    n = len(shape)
    return pl.BlockSpec(shape, lambda *_: (0,) * n, pipeline_mode=pl.Buffered(1))


def _proj_kernel(x_ref, an_ref, wq_ref, wkv_ref, wpe_ref, wmq_ref, wmk_ref, wmv_ref, wg_ref,
                 bg_ref, qn_ref, wuqn_ref, wuqr_ref, kvn_ref, wuk_ref, wuv_ref,
                 cra_ref, sra_ref, cta_ref, sta_ref, sga_ref,
                 crb_ref, srb_ref, ctb_ref, stb_ref, sgb_ref,
                 qa_ref, ka_ref, va_ref, qb_ref, kb_ref, vb_ref, g_ref, kmean_ref):
    hb = _rms(x_ref[0], an_ref[...]).astype(BF16)
    lane = lax.broadcasted_iota(jnp.int32, (hb.shape[0], LANES), 1)
    low_half = lane < MLA_ROPE
    first32 = (lane % MLA_ROPE) < (MLA_ROPE // 2)

    def rope_table(refs):
        cr_ref, sr_ref, ct_ref, st_ref, sign_ref = refs
        tile = pl.ds(pl.program_id(1), 1)
        cr, sr, ct, st = cr_ref[...], sr_ref[...], ct_ref[tile, :], st_ref[tile, :]
        return ct * cr - st * sr, (st * cr + ct * sr) * sign_ref[...]

    cosa, sina = rope_table((cra_ref, sra_ref, cta_ref, sta_ref, sga_ref))
    cosb, sinb = rope_table((crb_ref, srb_ref, ctb_ref, stb_ref, sgb_ref))
    ones_rows = jnp.ones((DV_AUG - LANES, hb.shape[0]), BF16)

    def rope64(c):
        partner = jnp.where(first32, pltpu.roll(c, LANES - 32, 1), pltpu.roll(c, 32, 1))
        return c * cosa + partner * sina

    def rope128(c):
        return c * cosb + pltpu.roll(c, LANES // 2, 1) * sinb

    tm = hb.shape[0]
    nb = kmean_ref.shape[1]
    blocks_per_tile = tm // MOBA_BLOCK
    si = pl.program_id(1)

    @pl.when(si == 0)
    def _():
        kmean_ref[...] = jnp.zeros_like(kmean_ref)

    row_blk = lax.broadcasted_iota(jnp.int32, (tm, LANES), 0) // MOBA_BLOCK
    onehot = jnp.where(lane == si * blocks_per_tile + row_blk, 1.0, 0.0).astype(BF16)
    mk = _dot(hb, wmk_ref[...])
    for h in range(MOBA_HEADS):
        kh = rope128(mk[:, h * LANES:(h + 1) * LANES])
        kb_ref[0, h, :, 0:LANES] = kh.astype(BF16)
        kb_ref[0, h, :, LANES:] = onehot
        kmean_ref[h, pl.ds(si * blocks_per_tile, blocks_per_tile), :] = jnp.mean(
            kh.reshape(blocks_per_tile, MOBA_BLOCK, LANES), axis=1)

    kblk = lax.broadcasted_iota(jnp.int32, (nb, tm), 0)
    kblk_f = kblk.astype(F32)
    qblk = si * blocks_per_tile + lax.broadcasted_iota(jnp.int32, (nb, tm), 1) // MOBA_BLOCK
    past = kblk < qblk
    mq = _dot(hb, wmq_ref[...])
    for h in range(MOBA_HEADS):
        qh = rope128(mq[:, h * LANES:(h + 1) * LANES])
        gt = lax.dot_general(kmean_ref[h], qh, (((1,), (1,)), ((), ())),
                             precision=lax.Precision.HIGHEST, preferred_element_type=F32)
        g = jnp.where(past, gt, -jnp.inf)
        sel = kblk == qblk
        for _ in range(MOBA_TOPK):
            first = jnp.min(jnp.where(g == jnp.max(g, axis=0, keepdims=True), kblk_f, float(nb)),
                            axis=0, keepdims=True)
            hit = kblk_f == first
            sel = sel | (hit & past)
            g = jnp.where(hit, -jnp.inf, g)
        bias_t = jnp.where(sel, 0.0, NEG)
        bias_t = jnp.concatenate([bias_t, jnp.zeros((LANES - nb, tm), F32)], axis=0)
        qb_ref[0, h, :, 0:LANES] = (qh * (MOBA_HD ** -0.5 * LOG2E)).astype(BF16)
        qb_ref[0, h, :, LANES:] = bias_t.T.astype(BF16)

    ql = _rms(_dot(hb, wq_ref[...]), qn_ref[...]).astype(BF16)
    scale_a = (MLA_NOPE + MLA_ROPE) ** -0.5 * LOG2E
    qn = _dot(ql, wuqn_ref[...]) * scale_a
    qr = _dot(ql, wuqr_ref[...]) * scale_a
    for h in range(MLA_HEADS):
        qa_ref[0, h, :, 0:LANES] = qn[:, h * LANES:(h + 1) * LANES].astype(BF16)
    for c in range(MLA_HEADS // 2):
        r = rope64(qr[:, c * LANES:(c + 1) * LANES])
        qa_ref[0, 2 * c, :, LANES:] = jnp.where(low_half, r, 0.0).astype(BF16)
        qa_ref[0, 2 * c + 1, :, LANES:] = jnp.where(
            low_half, pltpu.roll(r, LANES // 2, 1), 0.0).astype(BF16)

    kvl = _rms(_dot(hb, wkv_ref[...]), kvn_ref[...]).astype(BF16)
    kn = _dot(kvl, wuk_ref[...])
    vt = _dot_nt(wuv_ref[...], kvl)
    krot = rope64(_dot(hb, wpe_ref[...])).astype(BF16)
    for h in range(MLA_HEADS):
        ka_ref[0, h, :, 0:LANES] = kn[:, h * LANES:(h + 1) * LANES].astype(BF16)
        ka_ref[0, h, :, LANES:] = krot
        va_ref[0, h, 0, 0:LANES] = vt[h * LANES:(h + 1) * LANES, :].astype(BF16)
        va_ref[0, h, 0, LANES:] = ones_rows

    mvt = _dot_nt(wmv_ref[...], hb)
    for h in range(MOBA_HEADS):
        vb_ref[0, h, 0, 0:LANES] = mvt[h * LANES:(h + 1) * LANES, :].astype(BF16)
        vb_ref[0, h, 0, LANES:] = ones_rows

    z = _dot(hb, wg_ref[...]) + bg_ref[...]
    g_ref[0] = (1.0 / (1.0 + jnp.exp(-z))).astype(BF16)


def _flash_causal(nq, T, q_rows, k_tile, vt_tile, o_ref, scr):
    ns = 2 * ITEMS_PER_STEP
    s_refs, x_refs, p_refs, a_refs = (scr[k * ns:(k + 1) * ns] for k in range(4))
    m_all, acc_all = scr[4 * ns:4 * ns + 2]

    groups = range(T // QCOLS)
    cols = lambda c: slice(c * QCOLS, (c + 1) * QCOLS)

    def keys(diag, c):
        return (c + 1) * QCOLS if diag else T

    def scores(i, kt, slot, diag, c, ng):
        n, width = keys(diag, c), ng * QCOLS
        span = slice(c * QCOLS, c * QCOLS + width)
        s = _dot_nt(k_tile(kt, n), q_rows(i * T + c * QCOLS, width))
        if diag:
            key = lax.broadcasted_iota(jnp.int32, (n, width), 0)
            qry = lax.broadcasted_iota(jnp.int32, (n, width), 1) + c * QCOLS
            s = jnp.where(key <= qry, s, NEG)
        s_refs[slot][0:n, span] = s
        x_refs[slot][:, span] = jnp.broadcast_to(jnp.max(s, axis=0, keepdims=True),
                                                 (SUBLANES, width))

    def softmax(i, slot, diag, c):
        if diag:
            m_new = x_refs[slot][:, cols(c)]
        else:
            m_old = m_all[i, :, cols(c)]
            m_new = jnp.maximum(m_old, x_refs[slot][:, cols(c)])
            alpha = jnp.exp2(m_old - m_new)
            a_refs[slot][:, cols(c)] = alpha
        for r in range(keys(diag, c) // BF16_ROWS):
            rows = slice(r * BF16_ROWS, (r + 1) * BF16_ROWS)
            p = jnp.exp2(s_refs[slot][rows, cols(c)] - m_new[0:1, :])
            p_refs[slot][rows, cols(c)] = p.astype(BF16)
        m_all[i, :, cols(c)] = m_new

    def pv(i, kt, slot, diag, c):
        n = keys(diag, c)
        r = _dot(vt_tile(kt, n), p_refs[slot][0:n, cols(c)])
        acc_all[i, :, cols(c)] = (r if diag else
                                  a_refs[slot][0:1, cols(c)] * acc_all[i, :, cols(c)] + r)

    W = ITEMS_PER_STEP
    lo, hi = tuple(range(W)), tuple(range(W, 2 * W))

    def run_pass(n_items, coord, diag):
        assert n_items % W == 0 and n_items // W >= 3
        n_steps = n_items // W

        ng = 1 if diag else SCORE_GROUPS

        def block(j, do_s, do_x, do_v, xs, os):
            for c in groups:
                for w in range(W):
                    if do_s and c % ng == 0:
                        scores(*coord(W * (j + 1) + w), os[w], diag, c, ng)
                for w in range(W):
                    if do_x:
                        softmax(coord(W * j + w)[0], xs[w], diag, c)
                for w in range(W):
                    if do_v:
                        pv(*coord(W * (j - 1) + w), os[w], diag, c)

        def body(j, carry):
            lax.cond(j % 2 == 0, lambda: block(j, True, True, True, lo, hi),
                     lambda: block(j, True, True, True, hi, lo))
            return carry

        slots = lambda j: (lo, hi) if j % 2 == 0 else (hi, lo)
        block(-1, True, False, False, *slots(-1))
        block(0, True, True, False, *slots(0))
        lax.fori_loop(1, n_steps - 1, body, 0)
        block(n_steps - 1, False, True, True, *slots(n_steps - 1))
        block(n_steps, False, False, True, *slots(n_steps))

    run_pass(nq, lambda n: (n, n), True)

    def full_coord(n):
        i = 1 + sum((n >= t * (t - 1) // 2) * 1 for t in range(2, nq))
        return i, n - (i * (i - 1)) // 2

    run_pass(nq * (nq - 1) // 2, full_coord, False)

    def finish(i, c):
        acc = acc_all[i]
        o = acc[0:LANES] / acc[LANES:LANES + 1]
        o_ref[0, pl.ds(pl.multiple_of(i * T, T), T), :] = o.T.astype(BF16)
        return c

    lax.fori_loop(0, nq, finish, 0)


def _flash_scratch(nq, T, dv):
    ns = 2 * ITEMS_PER_STEP
    return ([pltpu.VMEM((T, T), F32)] * ns + [pltpu.VMEM((SUBLANES, T), F32)] * ns
            + [pltpu.VMEM((T, T), BF16)] * ns + [pltpu.VMEM((SUBLANES, T), F32)] * ns
            + [pltpu.VMEM((nq, SUBLANES, T), F32), pltpu.VMEM((nq, dv, T), F32)])


def _q_rows(ref):
    return lambda start, width: ref[pl.ds(pl.multiple_of(start, QCOLS), width), :]


def _k_tile(ref, T):
    return lambda kt, n: ref[pl.ds(pl.multiple_of(kt * T, T), n), :]


def _vt_tile(ref):
    return lambda kt, n: ref[0, 0, kt, :, 0:n]


def _attn_kernel(q_ref, k_ref, vt_ref, o_ref, *scr, nq, T):
    _flash_causal(nq, T, _q_rows(q_ref.at[0, 0]), _k_tile(k_ref.at[0, 0], T),
                  _vt_tile(vt_ref), o_ref, scr)


def _ffn_kernel(x_ref, ya_ref, yb_ref, g_ref, woa_ref, wob_ref, wout_ref,
                fn_ref, wup_ref, cw_ref, cb_ref, wdn_ref, gn_ref, o_ref, carry_ref,
                act_ref, *bufs, tm, n_chunks):
    si = pl.program_id(1)

    @pl.when(si == 0)
    def _():
        carry_ref[...] = jnp.zeros_like(carry_ref)

    g = g_ref[0].astype(F32)
    mix = (g[:, :D_MODEL] * _dot(ya_ref[0], woa_ref[...])
           + g[:, D_MODEL:] * _dot(yb_ref[0], wob_ref[...]))
    x = x_ref[0] + _dot(mix.astype(BF16), wout_ref[...])
    hb = _rms(x, fn_ref[...]).astype(BF16)

    def conv(u, buf, idx, col0):
        buf[0:SUBLANES, :] = carry_ref[idx]
        buf[SUBLANES:, :] = u
        carry_ref[idx] = u[tm - SUBLANES:, :]
        u1 = buf[SUBLANES - 1:SUBLANES - 1 + tm, :]
        u2 = buf[SUBLANES - 2:SUBLANES - 2 + tm, :]
        w = cw_ref[:, col0:col0 + FF_CHUNK]
        return cb_ref[:, col0:col0 + FF_CHUNK] + w[0:1] * u2 + w[1:2] * u1 + w[2:3] * u

    for c in range(n_chunks):
        ca = c * FF_CHUNK
        cb = D_FF + c * FF_CHUNK
        ya = conv(_dot(hb, wup_ref[:, ca:ca + FF_CHUNK]), bufs[2 * (c % 2)], c, ca)
        yb = conv(_dot(hb, wup_ref[:, cb:cb + FF_CHUNK]), bufs[2 * (c % 2) + 1], n_chunks + c, cb)
        act_ref[:, ca:ca + FF_CHUNK] = (ya * (1.0 / (1.0 + jnp.exp(-ya))) * yb).astype(BF16)
    o_ref[0] = _rms(x + _dot(act_ref[...], wdn_ref[...]), gn_ref[...])


def _rope_tables(seq, d, tm):
    inv = ROPE_THETA ** (-jnp.arange(0, d, 2, dtype=F32) / d)
    lanes = lambda t: jnp.tile(jnp.concatenate([t, t], axis=-1), (1, LANES // d))
    row_ang = jnp.arange(tm).astype(F32)[:, None] * inv[None, :]
    tile_ang = (jnp.arange(seq // tm) * tm).astype(F32)[:, None] * inv[None, :]
    half = jnp.ones((1, d // 2), F32)
    sign = jnp.tile(jnp.concatenate([-half, half], axis=-1), (1, LANES // d))
    return [lanes(jnp.cos(row_ang)), lanes(jnp.sin(row_ang)),
            lanes(jnp.cos(tile_ang)), lanes(jnp.sin(tile_ang)), sign]


def _params(n_axes, vmem_limit=VMEM_LIMIT):
    return pltpu.CompilerParams(dimension_semantics=("arbitrary",) * n_axes,
                                vmem_limit_bytes=vmem_limit)


def _layer(x, attn_norm, w_in, b_gate, q_norm, w_uq, kv_norm, w_ukv, w_o_mla, w_o_moba,
           w_out, ffn_norm, w_up, conv_w, conv_b, w_down, out_gain):
    B, S, D = x.shape
    H = MLA_HEADS
    T = ATT_TILE
    assert D == D_MODEL and S % T == 0 and S // T >= 3
    assert D_FF % FF_CHUNK == 0 and TM_PROJ % MOBA_BLOCK == 0 and T % TM_PROJ == 0
    nq = S // T
    nb = S // MOBA_BLOCK
    assert nb <= LANES and nb % SUBLANES == 0

    wcols = lambda a, b: w_in[:, a:b].astype(BF16)
    wq = wcols(OFF_QLAT, OFF_KVLAT)
    wkv = wcols(OFF_KVLAT, OFF_KPE)
    wpe = jnp.pad(wcols(OFF_KPE, OFF_MOBA), ((0, 0), (0, LANES - MLA_ROPE)))
    hd = MOBA_HEADS * MOBA_HD
    wmq = wcols(OFF_MOBA, OFF_MOBA + hd)
    wmk = wcols(OFF_MOBA + hd, OFF_MOBA + 2 * hd)
    wmv = wcols(OFF_MOBA + 2 * hd, OFF_GATE).T
    wg = wcols(OFF_GATE, OFF_GATE + N_BRANCH * D_MODEL)
    wuq3 = w_uq.astype(BF16).reshape(Q_LORA, H, MLA_NOPE + MLA_ROPE)
    wuqn = wuq3[:, :, :MLA_NOPE].reshape(Q_LORA, H * MLA_NOPE)
    wuqr = wuq3[:, :, MLA_NOPE:].reshape(Q_LORA, H * MLA_ROPE)
    wukv3 = w_ukv.astype(BF16).reshape(KV_LORA, H, MLA_NOPE + MLA_V)
    wuk = wukv3[:, :, :MLA_NOPE].reshape(KV_LORA, H * MLA_NOPE)
    wuv = wukv3[:, :, MLA_NOPE:].reshape(KV_LORA, H * MLA_V).T
    row2 = lambda v: v.reshape(1, -1)
    tm = TM_PROJ
    ropes = _rope_tables(S, MLA_ROPE, tm) + _rope_tables(S, MOBA_HD, tm)
    head_out = lambda w, dt: jax.ShapeDtypeStruct((B, H, S, w), dt)
    head_spec = lambda w: pl.BlockSpec((1, H, tm, w), lambda b, s: (b, 0, s, 0))
    sub = T // tm
    assert MLA_V == LANES and MOBA_HD == LANES
    vt_out = lambda w: jax.ShapeDtypeStruct((B, H, nq, DV_AUG, T), BF16)
    vt_spec = lambda w: pl.BlockSpec((1, H, 1, DV_AUG, tm),
                                     lambda b, s: (b, 0, s // sub, 0, s % sub))
    weights = [row2(attn_norm), wq, wkv, wpe, wmq, wmk, wmv, wg, row2(b_gate), row2(q_norm),
               wuqn, wuqr, row2(kv_norm), wuk, wuv]
    qa, ka, va, qb, kb, vb, g = pl.pallas_call(
        _proj_kernel,
        grid=(B, S // tm),
        in_specs=[pl.BlockSpec((1, tm, D), lambda b, s: (b, s, 0))]
        + [_const_spec(w.shape) for w in weights + ropes],
        out_specs=[head_spec(ATT_QK), head_spec(ATT_QK), vt_spec(MLA_V),
                   head_spec(ATT_QK), head_spec(ATT_QK), vt_spec(MOBA_HD),
                   pl.BlockSpec((1, tm, N_BRANCH * D), lambda b, s: (b, s, 0))],
        out_shape=[head_out(ATT_QK, BF16), head_out(ATT_QK, BF16), vt_out(MLA_V),
                   head_out(ATT_QK, BF16), head_out(ATT_QK, BF16), vt_out(MOBA_HD),
                   jax.ShapeDtypeStruct((B, S, N_BRANCH * D), BF16)],
        scratch_shapes=[pltpu.VMEM((H, nb, LANES), F32)],
        compiler_params=_params(2),
        name="proj",
    )(x, *weights, *ropes)

    seq_spec = lambda w, **kw: pl.BlockSpec((1, 1, S, w), lambda b, h: (b, h, 0, 0), **kw)
    vt_in = lambda w: pl.BlockSpec((1, 1, nq, DV_AUG, T), lambda b, h: (b, h, 0, 0, 0))
    att_out = pl.BlockSpec((1, S, LANES), lambda b, h: (b, 0, h))
    def attention(name, q, k, vt):
        return pl.pallas_call(
            functools.partial(_attn_kernel, nq=nq, T=T),
            grid=(B, H),
            in_specs=[seq_spec(ATT_QK), seq_spec(ATT_QK), vt_in(LANES)],
            out_specs=att_out,
            out_shape=jax.ShapeDtypeStruct((B, S, H * LANES), BF16),
            scratch_shapes=_flash_scratch(nq, T, DV_AUG),
            compiler_params=_params(2, ATT_VMEM_LIMIT),
            name=name,
        )(q, k, vt)

    ya = attention("mla_attn", qa, ka, va)
    yb = attention("moba_attn", qb, kb, vb)

    tm = TM_FFN
    n_chunks = D_FF // FF_CHUNK
    rows = lambda w: pl.BlockSpec((1, tm, w), lambda b, s: (b, s, 0))
    ffn_w = [w_o_mla.astype(BF16), w_o_moba.astype(BF16), w_out.astype(BF16),
             row2(ffn_norm), w_up.astype(BF16), conv_w, row2(conv_b), w_down.astype(BF16),
             row2(out_gain)]
    out = pl.pallas_call(
        functools.partial(_ffn_kernel, tm=tm, n_chunks=n_chunks),
        grid=(B, S // tm),
        in_specs=[rows(D), rows(D), rows(D), rows(N_BRANCH * D)]
        + [_const_spec(w.shape) for w in ffn_w],
        out_specs=pl.BlockSpec((1, tm, D), lambda b, s: (b, s, 0)),
        out_shape=jax.ShapeDtypeStruct((B, S, D), F32),
        scratch_shapes=[pltpu.VMEM((2 * n_chunks, SUBLANES, FF_CHUNK), F32),
                        pltpu.VMEM((tm, D_FF), BF16)]
        + [pltpu.VMEM((tm + SUBLANES, FF_CHUNK), F32)] * 4,
        compiler_params=_params(2),
        name="ffn",
    )(x, ya, yb, g, *ffn_w)
    return out


def kernel(x, attn_norm, w_in, b_gate, q_norm, w_uq, kv_norm, w_ukv, w_o_mla, w_o_moba,
           w_out, ffn_norm, w_up, conv_w, conv_b, w_down, final_norm):
    depth = w_in.shape[0]
    assert depth == 1
    return _layer(x, attn_norm[0], w_in[0], b_gate[0], q_norm[0], w_uq[0], kv_norm[0],
                  w_ukv[0], w_o_mla[0], w_o_moba[0], w_out[0], ffn_norm[0], w_up[0],
                  conv_w[0], conv_b[0], w_down[0], final_norm)
```

```python
import functools

import jax
import jax.numpy as jnp
from jax import lax
from jax.experimental import pallas as pl
from jax.experimental.pallas import tpu as pltpu

F32 = jnp.float32
BF16 = jnp.bfloat16

D_MODEL = 1024
MLA_HEADS = 8
MLA_NOPE = 128
MLA_ROPE = 64
MLA_V = 128
Q_LORA = 384
KV_LORA = 256
MOBA_HEADS = 8
MOBA_HD = 128
MOBA_BLOCK = 256
MOBA_TOPK = 3
D_FF = 2816
CONV_W = 3
ROPE_THETA = 10000.0
EPS = 1e-6
NEG = -1e30
N_BRANCH = 2
LOG2E = 1.4426950408889634

OFF_QLAT = 0
OFF_KVLAT = OFF_QLAT + Q_LORA
OFF_KPE = OFF_KVLAT + KV_LORA
OFF_MOBA = OFF_KPE + MLA_ROPE
OFF_GATE = OFF_MOBA + 3 * MOBA_HEADS * MOBA_HD

LANES = 128
SUBLANES = 8
BF16_ROWS = 16
ATT_QK = 2 * LANES
VMEM_LIMIT = 56 * 1024 * 1024

TM_PROJ = 512
ATT_TILE = 1024
QCOLS = 256
SCORE_GROUPS = 4
ATT_VMEM_LIMIT = 60 * 1024 * 1024
DV_AUG = LANES + BF16_ROWS
TM_FFN = 512
FF_CHUNK = 256


def _dot(a, b):
    return jnp.dot(a, b, preferred_element_type=F32)


def _dot_nt(a, b):
    return lax.dot_general(a, b, (((1,), (1,)), ((), ())), preferred_element_type=F32)


def _rms(x, g):
    return x * lax.rsqrt(jnp.mean(x * x, axis=-1, keepdims=True) + EPS) * g


def _const_spec(shape):
    n = len(shape)
    return pl.BlockSpec(shape, lambda *_: (0,) * n, pipeline_mode=pl.Buffered(1))


def _proj_kernel(x_ref, an_ref, wq_ref, wkv_ref, wpe_ref, wmq_ref, wmk_ref, wmv_ref, wg_ref,
                 bg_ref, qn_ref, wuqn_ref, wuqr_ref, kvn_ref, wuk_ref, wuv_ref,
                 cra_ref, sra_ref, cta_ref, sta_ref, sga_ref,
                 crb_ref, srb_ref, ctb_ref, stb_ref, sgb_ref,
                 qa_ref, ka_ref, va_ref, qb_ref, kb_ref, vb_ref, g_ref, kmean_ref):
    hb = _rms(x_ref[0], an_ref[...]).astype(BF16)
    lane = lax.broadcasted_iota(jnp.int32, (hb.shape[0], LANES), 1)
    low_half = lane < MLA_ROPE
    first32 = (lane % MLA_ROPE) < (MLA_ROPE // 2)

    def rope_table(refs):
        cr_ref, sr_ref, ct_ref, st_ref, sign_ref = refs
        tile = pl.ds(pl.program_id(1), 1)
        cr, sr, ct, st = cr_ref[...], sr_ref[...], ct_ref[tile, :], st_ref[tile, :]
        return ct * cr - st * sr, (st * cr + ct * sr) * sign_ref[...]

    cosa, sina = rope_table((cra_ref, sra_ref, cta_ref, sta_ref, sga_ref))
    cosb, sinb = rope_table((crb_ref, srb_ref, ctb_ref, stb_ref, sgb_ref))
    ones_rows = jnp.ones((DV_AUG - LANES, hb.shape[0]), BF16)

    def rope64(c):
        partner = jnp.where(first32, pltpu.roll(c, LANES - 32, 1), pltpu.roll(c, 32, 1))
        return c * cosa + partner * sina

    def rope128(c):
        return c * cosb + pltpu.roll(c, LANES // 2, 1) * sinb

    tm = hb.shape[0]
    nb = kmean_ref.shape[1]
    blocks_per_tile = tm // MOBA_BLOCK
    si = pl.program_id(1)

    @pl.when(si == 0)
    def _():
        kmean_ref[...] = jnp.zeros_like(kmean_ref)

    row_blk = lax.broadcasted_iota(jnp.int32, (tm, LANES), 0) // MOBA_BLOCK
    onehot = jnp.where(lane == si * blocks_per_tile + row_blk, 1.0, 0.0).astype(BF16)
    mk = _dot(hb, wmk_ref[...])
    for h in range(MOBA_HEADS):
        kh = rope128(mk[:, h * LANES:(h + 1) * LANES])
        kb_ref[0, h, :, 0:LANES] = kh.astype(BF16)
        kb_ref[0, h, :, LANES:] = onehot
        kmean_ref[h, pl.ds(si * blocks_per_tile, blocks_per_tile), :] = jnp.mean(
            kh.reshape(blocks_per_tile, MOBA_BLOCK, LANES), axis=1)

    kblk = lax.broadcasted_iota(jnp.int32, (nb, tm), 0)
    kblk_f = kblk.astype(F32)
    qblk = si * blocks_per_tile + lax.broadcasted_iota(jnp.int32, (nb, tm), 1) // MOBA_BLOCK
    past = kblk < qblk
    mq = _dot(hb, wmq_ref[...])
    for h in range(MOBA_HEADS):
        qh = rope128(mq[:, h * LANES:(h + 1) * LANES])
        gt = lax.dot_general(kmean_ref[h], qh, (((1,), (1,)), ((), ())),
                             precision=lax.Precision.HIGHEST, preferred_element_type=F32)
        g = jnp.where(past, gt, -jnp.inf)
        sel = kblk == qblk
        for _ in range(MOBA_TOPK):
            first = jnp.min(jnp.where(g == jnp.max(g, axis=0, keepdims=True), kblk_f, float(nb)),
                            axis=0, keepdims=True)
            hit = kblk_f == first
            sel = sel | (hit & past)
            g = jnp.where(hit, -jnp.inf, g)
        bias_t = jnp.where(sel, 0.0, NEG)
        bias_t = jnp.concatenate([bias_t, jnp.zeros((LANES - nb, tm), F32)], axis=0)
        qb_ref[0, h, :, 0:LANES] = (qh * (MOBA_HD ** -0.5 * LOG2E)).astype(BF16)
        qb_ref[0, h, :, LANES:] = bias_t.T.astype(BF16)

    ql = _rms(_dot(hb, wq_ref[...]), qn_ref[...]).astype(BF16)
    scale_a = (MLA_NOPE + MLA_ROPE) ** -0.5 * LOG2E
    qn = _dot(ql, wuqn_ref[...]) * scale_a
    qr = _dot(ql, wuqr_ref[...]) * scale_a
    for h in range(MLA_HEADS):
        qa_ref[0, h, :, 0:LANES] = qn[:, h * LANES:(h + 1) * LANES].astype(BF16)
    for c in range(MLA_HEADS // 2):
        r = rope64(qr[:, c * LANES:(c + 1) * LANES])
        qa_ref[0, 2 * c, :, LANES:] = jnp.where(low_half, r, 0.0).astype(BF16)
        qa_ref[0, 2 * c + 1, :, LANES:] = jnp.where(
            low_half, pltpu.roll(r, LANES // 2, 1), 0.0).astype(BF16)

    kvl = _rms(_dot(hb, wkv_ref[...]), kvn_ref[...]).astype(BF16)
    kn = _dot(kvl, wuk_ref[...])
    vt = _dot_nt(wuv_ref[...], kvl)
    krot = rope64(_dot(hb, wpe_ref[...])).astype(BF16)
    for h in range(MLA_HEADS):
        ka_ref[0, h, :, 0:LANES] = kn[:, h * LANES:(h + 1) * LANES].astype(BF16)
        ka_ref[0, h, :, LANES:] = krot
        va_ref[0, h, 0, 0:LANES] = vt[h * LANES:(h + 1) * LANES, :].astype(BF16)
        va_ref[0, h, 0, LANES:] = ones_rows

    mvt = _dot_nt(wmv_ref[...], hb)
    for h in range(MOBA_HEADS):
        vb_ref[0, h, 0, 0:LANES] = mvt[h * LANES:(h + 1) * LANES, :].astype(BF16)
        vb_ref[0, h, 0, LANES:] = ones_rows

    z = _dot(hb, wg_ref[...]) + bg_ref[...]
    g_ref[0] = (1.0 / (1.0 + jnp.exp(-z))).astype(BF16)


def _flash_causal(nq, T, q_rows, k_tile, vt_tile, o_ref, scr):
    s_refs, x_refs, p_refs, a_refs = scr[0:2], scr[2:4], scr[4:6], scr[6:8]
    m_all, acc_all = scr[8:10]

    groups = range(T // QCOLS)
    cols = lambda c: slice(c * QCOLS, (c + 1) * QCOLS)

    def keys(diag, c):
        return (c + 1) * QCOLS if diag else T

    def scores(i, kt, slot, diag, c, ng):
        n, width = keys(diag, c), ng * QCOLS
        span = slice(c * QCOLS, c * QCOLS + width)
        s = _dot_nt(k_tile(kt, n), q_rows(i * T + c * QCOLS, width))
        if diag:
            key = lax.broadcasted_iota(jnp.int32, (n, width), 0)
            qry = lax.broadcasted_iota(jnp.int32, (n, width), 1) + c * QCOLS
            s = jnp.where(key <= qry, s, NEG)
        s_refs[slot][0:n, span] = s
        x_refs[slot][:, span] = jnp.broadcast_to(jnp.max(s, axis=0, keepdims=True),
                                                 (SUBLANES, width))

    def softmax(i, slot, diag, c):
        if diag:
            m_new = x_refs[slot][:, cols(c)]
        else:
            m_old = m_all[i, :, cols(c)]
            m_new = jnp.maximum(m_old, x_refs[slot][:, cols(c)])
            alpha = jnp.exp2(m_old - m_new)
            a_refs[slot][:, cols(c)] = alpha
        for r in range(keys(diag, c) // BF16_ROWS):
            rows = slice(r * BF16_ROWS, (r + 1) * BF16_ROWS)
            p = jnp.exp2(s_refs[slot][rows, cols(c)] - m_new[0:1, :])
            p_refs[slot][rows, cols(c)] = p.astype(BF16)
        m_all[i, :, cols(c)] = m_new

    def pv(i, kt, slot, diag, c):
        n = keys(diag, c)
        r = _dot(vt_tile(kt, n), p_refs[slot][0:n, cols(c)])
        acc_all[i, :, cols(c)] = (r if diag else
                                  a_refs[slot][0:1, cols(c)] * acc_all[i, :, cols(c)] + r)

    n_all = nq + nq * (nq - 1) // 2
    D, F = True, False

    def coords(n, diag):
        if diag:
            return n, n
        m = n - nq
        i = 1 + sum((m >= t * (t - 1) // 2) * 1 for t in range(2, nq))
        return i, m - (i * (i - 1)) // 2

    def block(g, ks, kx, kv, parity=None):
        xs = g % 2 if parity is None else parity
        os = 1 - xs
        ng = 1 if ks else SCORE_GROUPS
        for c in groups:
            if ks is not None and c % ng == 0:
                scores(*coords(g + 1, ks), os, ks, c, ng)
            if kx is not None:
                softmax(coords(g, kx)[0], xs, kx, c)
            if kv is not None:
                pv(*coords(g - 1, kv), os, kv, c)

    def steps(first, last, kind):
        def body(g, carry):
            lax.cond(g % 2 == 0, lambda: block(g, kind, kind, kind, 0),
                     lambda: block(g, kind, kind, kind, 1))
            return carry

        lax.fori_loop(first, last + 1, body, 0)

    assert nq >= 4
    block(-1, D, None, None)
    block(0, D, D, None)
    steps(1, nq - 2, D)
    block(nq - 1, F, D, D)
    block(nq, F, F, D)
    steps(nq + 1, n_all - 2, F)
    block(n_all - 1, None, F, F)
    block(n_all, None, None, F)

    def finish(i, c):
        acc = acc_all[i]
        o = acc[0:LANES] / acc[LANES:LANES + 1]
        o_ref[0, pl.ds(pl.multiple_of(i * T, T), T), :] = o.T.astype(BF16)
        return c

    lax.fori_loop(0, nq, finish, 0)


def _flash_scratch(nq, T, dv):
    return ([pltpu.VMEM((T, T), F32)] * 2 + [pltpu.VMEM((SUBLANES, T), F32)] * 2
            + [pltpu.VMEM((T, T), BF16)] * 2 + [pltpu.VMEM((SUBLANES, T), F32)] * 2
            + [pltpu.VMEM((nq, SUBLANES, T), F32), pltpu.VMEM((nq, dv, T), F32)])


def _q_rows(ref):
    return lambda start, width: ref[pl.ds(pl.multiple_of(start, QCOLS), width), :]


def _k_tile(ref, T):
    return lambda kt, n: ref[pl.ds(pl.multiple_of(kt * T, T), n), :]


def _vt_tile(ref):
    return lambda kt, n: ref[0, 0, kt, :, 0:n]


def _attn_kernel(q_ref, k_ref, vt_ref, o_ref, *scr, nq, T):
    _flash_causal(nq, T, _q_rows(q_ref.at[0, 0]), _k_tile(k_ref.at[0, 0], T),
                  _vt_tile(vt_ref), o_ref, scr)


def _ffn_kernel(x_ref, ya_ref, yb_ref, g_ref, woa_ref, wob_ref, wout_ref,
                fn_ref, wup_ref, cw_ref, cb_ref, wdn_ref, gn_ref, o_ref, carry_ref,
                act_ref, *bufs, tm, n_chunks):
    si = pl.program_id(1)

    @pl.when(si == 0)
    def _():
        carry_ref[...] = jnp.zeros_like(carry_ref)

    g = g_ref[0].astype(F32)
    mix = (g[:, :D_MODEL] * _dot(ya_ref[0], woa_ref[...])
           + g[:, D_MODEL:] * _dot(yb_ref[0], wob_ref[...]))
    x = x_ref[0] + _dot(mix.astype(BF16), wout_ref[...])
    hb = _rms(x, fn_ref[...]).astype(BF16)

    def conv(u, buf, idx, col0):
        buf[0:SUBLANES, :] = carry_ref[idx]
        buf[SUBLANES:, :] = u
        carry_ref[idx] = u[tm - SUBLANES:, :]
        u1 = buf[SUBLANES - 1:SUBLANES - 1 + tm, :]
        u2 = buf[SUBLANES - 2:SUBLANES - 2 + tm, :]
        w = cw_ref[:, col0:col0 + FF_CHUNK]
        return cb_ref[:, col0:col0 + FF_CHUNK] + w[0:1] * u2 + w[1:2] * u1 + w[2:3] * u

    for c in range(n_chunks):
        ca = c * FF_CHUNK
        cb = D_FF + c * FF_CHUNK
        ya = conv(_dot(hb, wup_ref[:, ca:ca + FF_CHUNK]), bufs[2 * (c % 2)], c, ca)
        yb = conv(_dot(hb, wup_ref[:, cb:cb + FF_CHUNK]), bufs[2 * (c % 2) + 1], n_chunks + c, cb)
        act_ref[:, ca:ca + FF_CHUNK] = (ya * (1.0 / (1.0 + jnp.exp(-ya))) * yb).astype(BF16)
    o_ref[0] = _rms(x + _dot(act_ref[...], wdn_ref[...]), gn_ref[...])


def _rope_tables(seq, d, tm):
    inv = ROPE_THETA ** (-jnp.arange(0, d, 2, dtype=F32) / d)
    lanes = lambda t: jnp.tile(jnp.concatenate([t, t], axis=-1), (1, LANES // d))
    row_ang = jnp.arange(tm).astype(F32)[:, None] * inv[None, :]
    tile_ang = (jnp.arange(seq // tm) * tm).astype(F32)[:, None] * inv[None, :]
    half = jnp.ones((1, d // 2), F32)
    sign = jnp.tile(jnp.concatenate([-half, half], axis=-1), (1, LANES // d))
    return [lanes(jnp.cos(row_ang)), lanes(jnp.sin(row_ang)),
            lanes(jnp.cos(tile_ang)), lanes(jnp.sin(tile_ang)), sign]


def _params(n_axes, vmem_limit=VMEM_LIMIT):
    return pltpu.CompilerParams(dimension_semantics=("arbitrary",) * n_axes,
                                vmem_limit_bytes=vmem_limit)


def _layer(x, attn_norm, w_in, b_gate, q_norm, w_uq, kv_norm, w_ukv, w_o_mla, w_o_moba,
           w_out, ffn_norm, w_up, conv_w, conv_b, w_down, out_gain):
    B, S, D = x.shape
    H = MLA_HEADS
    T = ATT_TILE
    assert D == D_MODEL and S % T == 0 and S // T >= 3
    assert D_FF % FF_CHUNK == 0 and TM_PROJ % MOBA_BLOCK == 0 and T % TM_PROJ == 0
    nq = S // T
    nb = S // MOBA_BLOCK
    assert nb <= LANES and nb % SUBLANES == 0

    wcols = lambda a, b: w_in[:, a:b].astype(BF16)
    wq = wcols(OFF_QLAT, OFF_KVLAT)
    wkv = wcols(OFF_KVLAT, OFF_KPE)
    wpe = jnp.pad(wcols(OFF_KPE, OFF_MOBA), ((0, 0), (0, LANES - MLA_ROPE)))
    hd = MOBA_HEADS * MOBA_HD
    wmq = wcols(OFF_MOBA, OFF_MOBA + hd)
    wmk = wcols(OFF_MOBA + hd, OFF_MOBA + 2 * hd)
    wmv = wcols(OFF_MOBA + 2 * hd, OFF_GATE).T
    wg = wcols(OFF_GATE, OFF_GATE + N_BRANCH * D_MODEL)
    wuq3 = w_uq.astype(BF16).reshape(Q_LORA, H, MLA_NOPE + MLA_ROPE)
    wuqn = wuq3[:, :, :MLA_NOPE].reshape(Q_LORA, H * MLA_NOPE)
    wuqr = wuq3[:, :, MLA_NOPE:].reshape(Q_LORA, H * MLA_ROPE)
    wukv3 = w_ukv.astype(BF16).reshape(KV_LORA, H, MLA_NOPE + MLA_V)
    wuk = wukv3[:, :, :MLA_NOPE].reshape(KV_LORA, H * MLA_NOPE)
    wuv = wukv3[:, :, MLA_NOPE:].reshape(KV_LORA, H * MLA_V).T
    row2 = lambda v: v.reshape(1, -1)
    tm = TM_PROJ
    ropes = _rope_tables(S, MLA_ROPE, tm) + _rope_tables(S, MOBA_HD, tm)
    head_out = lambda w, dt: jax.ShapeDtypeStruct((B, H, S, w), dt)
    head_spec = lambda w: pl.BlockSpec((1, H, tm, w), lambda b, s: (b, 0, s, 0))
    sub = T // tm
    assert MLA_V == LANES and MOBA_HD == LANES
    vt_out = lambda w: jax.ShapeDtypeStruct((B, H, nq, DV_AUG, T), BF16)
    vt_spec = lambda w: pl.BlockSpec((1, H, 1, DV_AUG, tm),
                                     lambda b, s: (b, 0, s // sub, 0, s % sub))
    weights = [row2(attn_norm), wq, wkv, wpe, wmq, wmk, wmv, wg, row2(b_gate), row2(q_norm),
               wuqn, wuqr, row2(kv_norm), wuk, wuv]
    qa, ka, va, qb, kb, vb, g = pl.pallas_call(
        _proj_kernel,
        grid=(B, S // tm),
        in_specs=[pl.BlockSpec((1, tm, D), lambda b, s: (b, s, 0))]
        + [_const_spec(w.shape) for w in weights + ropes],
        out_specs=[head_spec(ATT_QK), head_spec(ATT_QK), vt_spec(MLA_V),
                   head_spec(ATT_QK), head_spec(ATT_QK), vt_spec(MOBA_HD),
                   pl.BlockSpec((1, tm, N_BRANCH * D), lambda b, s: (b, s, 0))],
        out_shape=[head_out(ATT_QK, BF16), head_out(ATT_QK, BF16), vt_out(MLA_V),
                   head_out(ATT_QK, BF16), head_out(ATT_QK, BF16), vt_out(MOBA_HD),
                   jax.ShapeDtypeStruct((B, S, N_BRANCH * D), BF16)],
        scratch_shapes=[pltpu.VMEM((H, nb, LANES), F32)],
        compiler_params=_params(2),
        name="proj",
    )(x, *weights, *ropes)

    seq_spec = lambda w, **kw: pl.BlockSpec((1, 1, S, w), lambda b, h: (b, h, 0, 0), **kw)
    vt_in = lambda w: pl.BlockSpec((1, 1, nq, DV_AUG, T), lambda b, h: (b, h, 0, 0, 0))
    att_out = pl.BlockSpec((1, S, LANES), lambda b, h: (b, 0, h))
    def attention(name, q, k, vt):
        return pl.pallas_call(
            functools.partial(_attn_kernel, nq=nq, T=T),
            grid=(B, H),
            in_specs=[seq_spec(ATT_QK), seq_spec(ATT_QK), vt_in(LANES)],
            out_specs=att_out,
            out_shape=jax.ShapeDtypeStruct((B, S, H * LANES), BF16),
            scratch_shapes=_flash_scratch(nq, T, DV_AUG),
            compiler_params=_params(2, ATT_VMEM_LIMIT),
            name=name,
        )(q, k, vt)

    ya = attention("mla_attn", qa, ka, va)
    yb = attention("moba_attn", qb, kb, vb)

    tm = TM_FFN
    n_chunks = D_FF // FF_CHUNK
    rows = lambda w: pl.BlockSpec((1, tm, w), lambda b, s: (b, s, 0))
    ffn_w = [w_o_mla.astype(BF16), w_o_moba.astype(BF16), w_out.astype(BF16),
             row2(ffn_norm), w_up.astype(BF16), conv_w, row2(conv_b), w_down.astype(BF16),
             row2(out_gain)]
    out = pl.pallas_call(
        functools.partial(_ffn_kernel, tm=tm, n_chunks=n_chunks),
        grid=(B, S // tm),
        in_specs=[rows(D), rows(D), rows(D), rows(N_BRANCH * D)]
        + [_const_spec(w.shape) for w in ffn_w],
        out_specs=pl.BlockSpec((1, tm, D), lambda b, s: (b, s, 0)),
        out_shape=jax.ShapeDtypeStruct((B, S, D), F32),
        scratch_shapes=[pltpu.VMEM((2 * n_chunks, SUBLANES, FF_CHUNK), F32),
                        pltpu.VMEM((tm, D_FF), BF16)]
        + [pltpu.VMEM((tm + SUBLANES, FF_CHUNK), F32)] * 4,
        compiler_params=_params(2),
        name="ffn",
    )(x, ya, yb, g, *ffn_w)
    return out


def kernel(x, attn_norm, w_in, b_gate, q_norm, w_uq, kv_norm, w_ukv, w_o_mla, w_o_moba,
           w_out, ffn_norm, w_up, conv_w, conv_b, w_down, final_norm):
    depth = w_in.shape[0]
    assert depth == 1
    return _layer(x, attn_norm[0], w_in[0], b_gate[0], q_norm[0], w_uq[0], kv_norm[0],
                  w_ukv[0], w_o_mla[0], w_o_moba[0], w_out[0], ffn_norm[0], w_up[0],
                  conv_w[0], conv_b[0], w_down[0], final_norm)
```

```python
import functools

import jax
import jax.numpy as jnp
from jax import lax
from jax.experimental import pallas as pl
from jax.experimental.pallas import tpu as pltpu

F32 = jnp.float32
BF16 = jnp.bfloat16

D_MODEL = 1024
MLA_HEADS = 8
MLA_NOPE = 128
MLA_ROPE = 64
MLA_V = 128
Q_LORA = 384
KV_LORA = 256
MOBA_HEADS = 8
MOBA_HD = 128
MOBA_BLOCK = 256
MOBA_TOPK = 3
D_FF = 2816
CONV_W = 3
ROPE_THETA = 10000.0
EPS = 1e-6
NEG = -1e30
N_BRANCH = 2
LOG2E = 1.4426950408889634

OFF_QLAT = 0
OFF_KVLAT = OFF_QLAT + Q_LORA
OFF_KPE = OFF_KVLAT + KV_LORA
OFF_MOBA = OFF_KPE + MLA_ROPE
OFF_GATE = OFF_MOBA + 3 * MOBA_HEADS * MOBA_HD

LANES = 128
SUBLANES = 8
BF16_ROWS = 16
ATT_QK = 2 * LANES
VMEM_LIMIT = 56 * 1024 * 1024

PK_Q = 0
PK_KV = PK_Q + Q_LORA
PK_PE = PK_KV + KV_LORA
PK_MQ = PK_PE + LANES
PK_MK = PK_MQ + MOBA_HEADS * MOBA_HD
PK_G = PK_MK + MOBA_HEADS * MOBA_HD
PK_END = PK_G + N_BRANCH * D_MODEL

TM_PROJ = 512
ATT_TILE = 1024
QCOLS = 256
SCORE_GROUPS = 4
PV_GROUPS = 4
DV_AUG = LANES + BF16_ROWS
TM_FFN = 512
FF_CHUNK = 256


def _dot(a, b):
    return jnp.dot(a, b, preferred_element_type=F32)


def _dot_nt(a, b):
    return lax.dot_general(a, b, (((1,), (1,)), ((), ())), preferred_element_type=F32)


def _rms(x, g):
    return x * lax.rsqrt(jnp.mean(x * x, axis=-1, keepdims=True) + EPS) * g


def _const_spec(shape):
    n = len(shape)
    return pl.BlockSpec(shape, lambda *_: (0,) * n, pipeline_mode=pl.Buffered(1))


def _proj_kernel(x_ref, an_ref, w_ref, wmv_ref,
                 bg_ref, qn_ref, wuqn_ref, wuqr_ref, kvn_ref, wuk_ref, wuv_ref,
                 cra_ref, sra_ref, cta_ref, sta_ref, sga_ref,
                 crb_ref, srb_ref, ctb_ref, stb_ref, sgb_ref,
                 qa_ref, ka_ref, va_ref, qb_ref, kb_ref, vb_ref, g_ref, kmean_ref):
    hb = _rms(x_ref[0], an_ref[...]).astype(BF16)
    lane = lax.broadcasted_iota(jnp.int32, (hb.shape[0], LANES), 1)
    low_half = lane < MLA_ROPE
    first32 = (lane % MLA_ROPE) < (MLA_ROPE // 2)

    def rope_table(refs):
        cr_ref, sr_ref, ct_ref, st_ref, sign_ref = refs
        tile = pl.ds(pl.program_id(1), 1)
        cr, sr, ct, st = cr_ref[...], sr_ref[...], ct_ref[tile, :], st_ref[tile, :]
        return ct * cr - st * sr, (st * cr + ct * sr) * sign_ref[...]

    cosa, sina = rope_table((cra_ref, sra_ref, cta_ref, sta_ref, sga_ref))
    cosb, sinb = rope_table((crb_ref, srb_ref, ctb_ref, stb_ref, sgb_ref))
    ones_rows = jnp.ones((DV_AUG - LANES, hb.shape[0]), BF16)

    def rope64(c):
        half = MLA_ROPE // 2
        partner = jnp.where(first32, pltpu.roll(c, LANES - half, 1), pltpu.roll(c, half, 1))
        return c * cosa + partner * sina

    def rope128(c):
        return c * cosb + pltpu.roll(c, LANES // 2, 1) * sinb

    tm = hb.shape[0]
    nb = kmean_ref.shape[1]
    blocks_per_tile = tm // MOBA_BLOCK
    si = pl.program_id(1)

    @pl.when(si == 0)
    def _():
        kmean_ref[...] = jnp.zeros_like(kmean_ref)

    row_blk = lax.broadcasted_iota(jnp.int32, (tm, LANES), 0) // MOBA_BLOCK
    onehot = jnp.where(lane == si * blocks_per_tile + row_blk, 1.0, 0.0).astype(BF16)
    mk = _dot(hb, w_ref[:, PK_MK:PK_G])
    for h in range(MOBA_HEADS):
        kh = rope128(mk[:, h * LANES:(h + 1) * LANES])
        kb_ref[0, h, :, 0:LANES] = kh.astype(BF16)
        kb_ref[0, h, :, LANES:] = onehot
        kmean_ref[h, pl.ds(si * blocks_per_tile, blocks_per_tile), :] = jnp.mean(
            kh.reshape(blocks_per_tile, MOBA_BLOCK, LANES), axis=1)

    kblk = lax.broadcasted_iota(jnp.int32, (nb, tm), 0)
    kblk_f = kblk.astype(F32)
    qblk = si * blocks_per_tile + lax.broadcasted_iota(jnp.int32, (nb, tm), 1) // MOBA_BLOCK
    past = kblk < qblk
    mq = _dot(hb, w_ref[:, PK_MQ:PK_MK])
    for h in range(MOBA_HEADS):
        qh = rope128(mq[:, h * LANES:(h + 1) * LANES])
        gt = lax.dot_general(kmean_ref[h], qh, (((1,), (1,)), ((), ())),
                             precision=lax.Precision.HIGHEST, preferred_element_type=F32)
        g = jnp.where(past, gt, -jnp.inf)
        sel = kblk == qblk
        for _ in range(MOBA_TOPK):
            first = jnp.min(jnp.where(g == jnp.max(g, axis=0, keepdims=True), kblk_f, float(nb)),
                            axis=0, keepdims=True)
            hit = kblk_f == first
            sel = sel | (hit & past)
            g = jnp.where(hit, -jnp.inf, g)
        bias_t = jnp.where(sel, 0.0, NEG)
        bias_t = jnp.concatenate([bias_t, jnp.zeros((LANES - nb, tm), F32)], axis=0)
        qb_ref[0, h, :, 0:LANES] = (qh * (MOBA_HD ** -0.5 * LOG2E)).astype(BF16)
        qb_ref[0, h, :, LANES:] = bias_t.T.astype(BF16)

    ql = _rms(_dot(hb, w_ref[:, PK_Q:PK_KV]), qn_ref[...]).astype(BF16)
    scale_a = (MLA_NOPE + MLA_ROPE) ** -0.5 * LOG2E
    qn = _dot(ql, wuqn_ref[...]) * scale_a
    qr = _dot(ql, wuqr_ref[...]) * scale_a
    for h in range(MLA_HEADS):
        qa_ref[0, h, :, 0:LANES] = qn[:, h * LANES:(h + 1) * LANES].astype(BF16)
    for c in range(MLA_HEADS // 2):
        r = rope64(qr[:, c * LANES:(c + 1) * LANES])
        qa_ref[0, 2 * c, :, LANES:] = jnp.where(low_half, r, 0.0).astype(BF16)
        qa_ref[0, 2 * c + 1, :, LANES:] = jnp.where(
            low_half, pltpu.roll(r, LANES // 2, 1), 0.0).astype(BF16)

    kvl = _rms(_dot(hb, w_ref[:, PK_KV:PK_PE]), kvn_ref[...]).astype(BF16)
    kn = _dot(kvl, wuk_ref[...])
    vt = _dot_nt(wuv_ref[...], kvl)
    krot = rope64(_dot(hb, w_ref[:, PK_PE:PK_MQ])).astype(BF16)
    for h in range(MLA_HEADS):
        ka_ref[0, h, :, 0:LANES] = kn[:, h * LANES:(h + 1) * LANES].astype(BF16)
        ka_ref[0, h, :, LANES:] = krot
        va_ref[0, h, 0, 0:LANES] = vt[h * LANES:(h + 1) * LANES, :].astype(BF16)
        va_ref[0, h, 0, LANES:] = ones_rows

    mvt = _dot_nt(wmv_ref[...], hb)
    for h in range(MOBA_HEADS):
        vb_ref[0, h, 0, 0:LANES] = mvt[h * LANES:(h + 1) * LANES, :].astype(BF16)
        vb_ref[0, h, 0, LANES:] = ones_rows

    z = _dot(hb, w_ref[:, PK_G:PK_END]) + bg_ref[...]
    g_ref[0] = (1.0 / (1.0 + jnp.exp(-z))).astype(BF16)


def _flash_causal(nq, T, q_rows, k_tile, vt_tile, o_ref, scr):
    s_refs, x_refs, p_refs, a_refs = scr[0:2], scr[2:4], scr[4:6], scr[6:8]
    m_all, acc_all = scr[8:10]

    groups = range(T // QCOLS)
    cols = lambda c: slice(c * QCOLS, (c + 1) * QCOLS)

    def keys(diag, c):
        return (c + 1) * QCOLS if diag else T

    def scores(i, kt, slot, diag, c, ng):
        n, width = keys(diag, c), ng * QCOLS
        span = slice(c * QCOLS, c * QCOLS + width)
        s = _dot_nt(k_tile(kt, n), q_rows(i * T + c * QCOLS, width))
        if diag:
            key = lax.broadcasted_iota(jnp.int32, (n, width), 0)
            qry = lax.broadcasted_iota(jnp.int32, (n, width), 1) + c * QCOLS
            s = jnp.where(key <= qry, s, NEG)
        s_refs[slot][0:n, span] = s
        x_refs[slot][:, span] = jnp.broadcast_to(jnp.max(s, axis=0, keepdims=True),
                                                 (SUBLANES, width))

    def softmax(i, slot, diag, c):
        if diag:
            m_new = x_refs[slot][:, cols(c)]
        else:
            m_old = m_all[i, :, cols(c)]
            m_new = jnp.maximum(m_old, x_refs[slot][:, cols(c)])
            alpha = jnp.exp2(m_old - m_new)
            a_refs[slot][:, cols(c)] = alpha
        for r in range(keys(diag, c) // BF16_ROWS):
            rows = slice(r * BF16_ROWS, (r + 1) * BF16_ROWS)
            p = jnp.exp2(s_refs[slot][rows, cols(c)] - m_new[0:1, :])
            p_refs[slot][rows, cols(c)] = p.astype(BF16)
        m_all[i, :, cols(c)] = m_new

    def pv(i, kt, slot, diag, c, ng):
        n = keys(diag, c)
        span = slice(c * QCOLS, (c + ng) * QCOLS)
        r = _dot(vt_tile(kt, n), p_refs[slot][0:n, span])
        acc_all[i, :, span] = (r if diag else
                               a_refs[slot][0:1, span] * acc_all[i, :, span] + r)

    n_all = nq + nq * (nq - 1) // 2
    D, F = True, False

    def coords(n, diag):
        if diag:
            return n, n
        m = n - nq
        i = 1 + sum((m >= t * (t - 1) // 2) * 1 for t in range(2, nq))
        return i, m - (i * (i - 1)) // 2

    def block(g, ks, kx, kv, parity=None):
        xs = g % 2 if parity is None else parity
        os = 1 - xs
        ngs = 1 if ks else SCORE_GROUPS
        ngv = 1 if kv else PV_GROUPS
        for c in groups:
            if ks is not None and c % ngs == 0:
                scores(*coords(g + 1, ks), os, ks, c, ngs)
            if kx is not None:
                softmax(coords(g, kx)[0], xs, kx, c)
            if kv is not None and (c + 1) % ngv == 0:
                pv(*coords(g - 1, kv), os, kv, c + 1 - ngv, ngv)

    def steps(first, last, kind):
        def body(g, carry):
            lax.cond(g % 2 == 0, lambda: block(g, kind, kind, kind, 0),
                     lambda: block(g, kind, kind, kind, 1))
            return carry

        lax.fori_loop(first, last + 1, body, 0)

    assert nq >= 4
    block(-1, D, None, None)
    block(0, D, D, None)
    steps(1, nq - 2, D)
    block(nq - 1, F, D, D)
    block(nq, F, F, D)
    steps(nq + 1, n_all - 2, F)
    block(n_all - 1, None, F, F)
    block(n_all, None, None, F)

    def finish(i, c):
        acc = acc_all[i]
        o = acc[0:LANES] / acc[LANES:LANES + 1]
        o_ref[0, pl.ds(pl.multiple_of(i * T, T), T), :] = o.T.astype(BF16)
        return c

    lax.fori_loop(0, nq, finish, 0)


def _flash_scratch(nq, T, dv):
    return ([pltpu.VMEM((T, T), F32)] * 2 + [pltpu.VMEM((SUBLANES, T), F32)] * 2
            + [pltpu.VMEM((T, T), BF16)] * 2 + [pltpu.VMEM((SUBLANES, T), F32)] * 2
            + [pltpu.VMEM((nq, SUBLANES, T), F32), pltpu.VMEM((nq, dv, T), F32)])


def _q_rows(ref):
    return lambda start, width: ref[pl.ds(pl.multiple_of(start, QCOLS), width), :]


def _k_tile(ref, T):
    return lambda kt, n: ref[pl.ds(pl.multiple_of(kt * T, T), n), :]


def _vt_tile(ref):
    return lambda kt, n: ref[0, 0, kt, :, 0:n]


def _attn_kernel(q_ref, k_ref, vt_ref, o_ref, *scr, nq, T):
    _flash_causal(nq, T, _q_rows(q_ref.at[0, 0]), _k_tile(k_ref.at[0, 0], T),
                  _vt_tile(vt_ref), o_ref, scr)


def _ffn_kernel(x_ref, ya_ref, yb_ref, g_ref, woa_ref, wob_ref, wout_ref,
                fn_ref, wup_ref, cw_ref, cb_ref, wdn_ref, gn_ref, o_ref, carry_ref,
                act_ref, *bufs, tm, n_chunks):
    si = pl.program_id(1)

    @pl.when(si == 0)
    def _():
        carry_ref[...] = jnp.zeros_like(carry_ref)

    g = g_ref[0].astype(F32)
    mix = (g[:, :D_MODEL] * _dot(ya_ref[0], woa_ref[...])
           + g[:, D_MODEL:] * _dot(yb_ref[0], wob_ref[...]))
    x = x_ref[0] + _dot(mix.astype(BF16), wout_ref[...])
    hb = _rms(x, fn_ref[...]).astype(BF16)

    def conv(u, buf, idx, col0):
        buf[0:SUBLANES, :] = carry_ref[idx]
        buf[SUBLANES:, :] = u
        carry_ref[idx] = u[tm - SUBLANES:, :]
        u1 = buf[SUBLANES - 1:SUBLANES - 1 + tm, :]
        u2 = buf[SUBLANES - 2:SUBLANES - 2 + tm, :]
        w = cw_ref[:, col0:col0 + FF_CHUNK]
        return cb_ref[:, col0:col0 + FF_CHUNK] + w[0:1] * u2 + w[1:2] * u1 + w[2:3] * u

    for c in range(n_chunks):
        ca = c * FF_CHUNK
        cb = D_FF + c * FF_CHUNK
        ya = conv(_dot(hb, wup_ref[:, ca:ca + FF_CHUNK]), bufs[2 * (c % 2)], c, ca)
        yb = conv(_dot(hb, wup_ref[:, cb:cb + FF_CHUNK]), bufs[2 * (c % 2) + 1], n_chunks + c, cb)
        act_ref[:, ca:ca + FF_CHUNK] = (ya * (1.0 / (1.0 + jnp.exp(-ya))) * yb).astype(BF16)
    o_ref[0] = _rms(x + _dot(act_ref[...], wdn_ref[...]), gn_ref[...])


def _rope_tables(seq, d, tm):
    inv = ROPE_THETA ** (-jnp.arange(0, d, 2, dtype=F32) / d)
    lanes = lambda t: jnp.tile(jnp.concatenate([t, t], axis=-1), (1, LANES // d))
    row_ang = jnp.arange(tm).astype(F32)[:, None] * inv[None, :]
    tile_ang = (jnp.arange(seq // tm) * tm).astype(F32)[:, None] * inv[None, :]
    half = jnp.ones((1, d // 2), F32)
    sign = jnp.tile(jnp.concatenate([-half, half], axis=-1), (1, LANES // d))
    return [lanes(jnp.cos(row_ang)), lanes(jnp.sin(row_ang)),
            lanes(jnp.cos(tile_ang)), lanes(jnp.sin(tile_ang)), sign]


def _params(n_axes):
    return pltpu.CompilerParams(dimension_semantics=("arbitrary",) * n_axes,
                                vmem_limit_bytes=VMEM_LIMIT)


def _layer(x, attn_norm, w_in, b_gate, q_norm, w_uq, kv_norm, w_ukv, w_o_mla, w_o_moba,
           w_out, ffn_norm, w_up, conv_w, conv_b, w_down, out_gain):
    B, S, D = x.shape
    H = MLA_HEADS
    T = ATT_TILE
    assert D == D_MODEL and S % T == 0 and S // T >= 3
    assert D_FF % FF_CHUNK == 0 and TM_PROJ % MOBA_BLOCK == 0 and T % TM_PROJ == 0
    nq = S // T
    nb = S // MOBA_BLOCK
    assert nb <= LANES and nb % SUBLANES == 0

    wcols = lambda a, b: w_in[:, a:b].astype(BF16)
    hd = MOBA_HEADS * MOBA_HD
    wpack = jnp.concatenate(
        [wcols(OFF_QLAT, OFF_KPE),
         jnp.pad(wcols(OFF_KPE, OFF_MOBA), ((0, 0), (0, LANES - MLA_ROPE))),
         wcols(OFF_MOBA, OFF_MOBA + 2 * hd),
         wcols(OFF_GATE, OFF_GATE + N_BRANCH * D_MODEL)], axis=1)
    assert wpack.shape[1] == PK_END
    wmv = wcols(OFF_MOBA + 2 * hd, OFF_GATE).T
    wuq3 = w_uq.astype(BF16).reshape(Q_LORA, H, MLA_NOPE + MLA_ROPE)
    wuqn = wuq3[:, :, :MLA_NOPE].reshape(Q_LORA, H * MLA_NOPE)
    wuqr = wuq3[:, :, MLA_NOPE:].reshape(Q_LORA, H * MLA_ROPE)
    wukv3 = w_ukv.astype(BF16).reshape(KV_LORA, H, MLA_NOPE + MLA_V)
    wuk = wukv3[:, :, :MLA_NOPE].reshape(KV_LORA, H * MLA_NOPE)
    wuv = wukv3[:, :, MLA_NOPE:].reshape(KV_LORA, H * MLA_V).T
    row2 = lambda v: v.reshape(1, -1)
    tm = TM_PROJ
    ropes = _rope_tables(S, MLA_ROPE, tm) + _rope_tables(S, MOBA_HD, tm)
    head_out = lambda w, dt: jax.ShapeDtypeStruct((B, H, S, w), dt)
    head_spec = lambda w: pl.BlockSpec((1, H, tm, w), lambda b, s: (b, 0, s, 0))
    sub = T // tm
    assert MLA_V == LANES and MOBA_HD == LANES
    vt_out = lambda w: jax.ShapeDtypeStruct((B, H, nq, DV_AUG, T), BF16)
    vt_spec = lambda w: pl.BlockSpec((1, H, 1, DV_AUG, tm),
                                     lambda b, s: (b, 0, s // sub, 0, s % sub))
    weights = [row2(attn_norm), wpack, wmv, row2(b_gate), row2(q_norm),
               wuqn, wuqr, row2(kv_norm), wuk, wuv]
    qa, ka, va, qb, kb, vb, g = pl.pallas_call(
        _proj_kernel,
        grid=(B, S // tm),
        in_specs=[pl.BlockSpec((1, tm, D), lambda b, s: (b, s, 0))]
        + [_const_spec(w.shape) for w in weights + ropes],
        out_specs=[head_spec(ATT_QK), head_spec(ATT_QK), vt_spec(MLA_V),
                   head_spec(ATT_QK), head_spec(ATT_QK), vt_spec(MOBA_HD),
                   pl.BlockSpec((1, tm, N_BRANCH * D), lambda b, s: (b, s, 0))],
        out_shape=[head_out(ATT_QK, BF16), head_out(ATT_QK, BF16), vt_out(MLA_V),
                   head_out(ATT_QK, BF16), head_out(ATT_QK, BF16), vt_out(MOBA_HD),
                   jax.ShapeDtypeStruct((B, S, N_BRANCH * D), BF16)],
        scratch_shapes=[pltpu.VMEM((H, nb, LANES), F32)],
        compiler_params=_params(2),
        name="proj",
    )(x, *weights, *ropes)

    seq_spec = lambda w, **kw: pl.BlockSpec((1, 1, S, w), lambda b, h: (b, h, 0, 0), **kw)
    vt_in = lambda w: pl.BlockSpec((1, 1, nq, DV_AUG, T), lambda b, h: (b, h, 0, 0, 0))
    att_out = pl.BlockSpec((1, S, LANES), lambda b, h: (b, 0, h))
    def attention(name, q, k, vt):
        return pl.pallas_call(
            functools.partial(_attn_kernel, nq=nq, T=T),
            grid=(B, H),
            in_specs=[seq_spec(ATT_QK), seq_spec(ATT_QK), vt_in(LANES)],
            out_specs=att_out,
            out_shape=jax.ShapeDtypeStruct((B, S, H * LANES), BF16),
            scratch_shapes=_flash_scratch(nq, T, DV_AUG),
            compiler_params=_params(2),
            name=name,
        )(q, k, vt)

    ya = attention("mla_attn", qa, ka, va)
    yb = attention("moba_attn", qb, kb, vb)

    tm = TM_FFN
    n_chunks = D_FF // FF_CHUNK
    rows = lambda w: pl.BlockSpec((1, tm, w), lambda b, s: (b, s, 0))
    ffn_w = [w_o_mla.astype(BF16), w_o_moba.astype(BF16), w_out.astype(BF16),
             row2(ffn_norm), w_up.astype(BF16), conv_w, row2(conv_b), w_down.astype(BF16),
             row2(out_gain)]
    out = pl.pallas_call(
        functools.partial(_ffn_kernel, tm=tm, n_chunks=n_chunks),
        grid=(B, S // tm),
        in_specs=[rows(D), rows(D), rows(D), rows(N_BRANCH * D)]
        + [_const_spec(w.shape) for w in ffn_w],
        out_specs=pl.BlockSpec((1, tm, D), lambda b, s: (b, s, 0)),
        out_shape=jax.ShapeDtypeStruct((B, S, D), F32),
        scratch_shapes=[pltpu.VMEM((2 * n_chunks, SUBLANES, FF_CHUNK), F32),
                        pltpu.VMEM((tm, D_FF), BF16)]
        + [pltpu.VMEM((tm + SUBLANES, FF_CHUNK), F32)] * 4,
        compiler_params=_params(2),
        name="ffn",
    )(x, ya, yb, g, *ffn_w)
    return out


def kernel(x, attn_norm, w_in, b_gate, q_norm, w_uq, kv_norm, w_ukv, w_o_mla, w_o_moba,
           w_out, ffn_norm, w_up, conv_w, conv_b, w_down, final_norm):
    depth = w_in.shape[0]
    assert depth == 1
    return _layer(x, attn_norm[0], w_in[0], b_gate[0], q_norm[0], w_uq[0], kv_norm[0],
                  w_ukv[0], w_o_mla[0], w_o_moba[0], w_out[0], ffn_norm[0], w_up[0],
                  conv_w[0], conv_b[0], w_down[0], final_norm)
```

```python
import functools

import jax
import jax.numpy as jnp
from jax import lax
from jax.experimental import pallas as pl
from jax.experimental.pallas import tpu as pltpu

F32 = jnp.float32
BF16 = jnp.bfloat16

D_MODEL = 1024
MLA_HEADS = 8
MLA_NOPE = 128
MLA_ROPE = 64
MLA_V = 128
Q_LORA = 384
KV_LORA = 256
MOBA_HEADS = 8
MOBA_HD = 128
MOBA_BLOCK = 256
MOBA_TOPK = 3
D_FF = 2816
CONV_W = 3
ROPE_THETA = 10000.0
EPS = 1e-6
NEG = -1e30
N_BRANCH = 2
LOG2E = 1.4426950408889634

OFF_QLAT = 0
OFF_KVLAT = OFF_QLAT + Q_LORA
OFF_KPE = OFF_KVLAT + KV_LORA
OFF_MOBA = OFF_KPE + MLA_ROPE
OFF_GATE = OFF_MOBA + 3 * MOBA_HEADS * MOBA_HD

LANES = 128
SUBLANES = 8
BF16_ROWS = 16
ATT_QK = 2 * LANES
VMEM_LIMIT = 56 * 1024 * 1024

TM_PROJ = 512
ATT_TILE = 1024
QCOLS = 256
SCORE_GROUPS = 4
PV_GROUPS = 4
DV_AUG = LANES + BF16_ROWS
TM_FFN = 512
FF_CHUNK = 256


def _dot(a, b):
    return jnp.dot(a, b, preferred_element_type=F32)


def _dot_nt(a, b):
    return lax.dot_general(a, b, (((1,), (1,)), ((), ())), preferred_element_type=F32)


def _split_bf16(x):
    hi = x.astype(BF16)
    return hi, (x - hi.astype(F32)).astype(BF16)


def _rms(x, g):
    return x * lax.rsqrt(jnp.mean(x * x, axis=-1, keepdims=True) + EPS) * g


def _const_spec(shape):
    n = len(shape)
    return pl.BlockSpec(shape, lambda *_: (0,) * n, pipeline_mode=pl.Buffered(1))


def _proj_kernel(x_ref, an_ref, wq_ref, wkv_ref, wpe_ref, wmq_ref, wmk_ref, wmv_ref, wg_ref,
                 bg_ref, qn_ref, wuqn_ref, wuqr_ref, kvn_ref, wuk_ref, wuv_ref,
                 cra_ref, sra_ref, cta_ref, sta_ref, sga_ref,
                 crb_ref, srb_ref, ctb_ref, stb_ref, sgb_ref,
                 qa_ref, ka_ref, va_ref, qb_ref, kb_ref, vb_ref, g_ref, kmean_ref):
    hb = _rms(x_ref[0], an_ref[...]).astype(BF16)
    lane = lax.broadcasted_iota(jnp.int32, (hb.shape[0], LANES), 1)
    low_half = lane < MLA_ROPE
    first32 = (lane % MLA_ROPE) < (MLA_ROPE // 2)

    def rope_table(refs):
        cr_ref, sr_ref, ct_ref, st_ref, sign_ref = refs
        tile = pl.ds(pl.program_id(1), 1)
        cr, sr, ct, st = cr_ref[...], sr_ref[...], ct_ref[tile, :], st_ref[tile, :]
        return ct * cr - st * sr, (st * cr + ct * sr) * sign_ref[...]

    cosa, sina = rope_table((cra_ref, sra_ref, cta_ref, sta_ref, sga_ref))
    cosb, sinb = rope_table((crb_ref, srb_ref, ctb_ref, stb_ref, sgb_ref))
    ones_rows = jnp.ones((DV_AUG - LANES, hb.shape[0]), BF16)

    def rope64(c):
        half = MLA_ROPE // 2
        partner = jnp.where(first32, pltpu.roll(c, LANES - half, 1), pltpu.roll(c, half, 1))
        return c * cosa + partner * sina

    def rope128(c):
        return c * cosb + pltpu.roll(c, LANES // 2, 1) * sinb

    tm = hb.shape[0]
    nb = kmean_ref.shape[1]
    blocks_per_tile = tm // MOBA_BLOCK
    si = pl.program_id(1)

    @pl.when(si == 0)
    def _():
        kmean_ref[...] = jnp.zeros_like(kmean_ref)

    row_blk = lax.broadcasted_iota(jnp.int32, (tm, LANES), 0) // MOBA_BLOCK
    onehot = jnp.where(lane == si * blocks_per_tile + row_blk, 1.0, 0.0).astype(BF16)
    mk = _dot(hb, wmk_ref[...])
    for h in range(MOBA_HEADS):
        kh = rope128(mk[:, h * LANES:(h + 1) * LANES])
        kb_ref[0, h, :, 0:LANES] = kh.astype(BF16)
        kb_ref[0, h, :, LANES:] = onehot
        kmean_ref[h, pl.ds(si * blocks_per_tile, blocks_per_tile), :] = jnp.mean(
            kh.reshape(blocks_per_tile, MOBA_BLOCK, LANES), axis=1)

    kblk = lax.broadcasted_iota(jnp.int32, (nb, tm), 0)
    kblk_f = kblk.astype(F32)
    qblk = si * blocks_per_tile + lax.broadcasted_iota(jnp.int32, (nb, tm), 1) // MOBA_BLOCK
    past = kblk < qblk
    mq = _dot(hb, wmq_ref[...])
    for h in range(MOBA_HEADS):
        qh = rope128(mq[:, h * LANES:(h + 1) * LANES])
        km_hi, km_lo = _split_bf16(kmean_ref[h])
        q_hi, q_lo = _split_bf16(qh)
        gt = _dot_nt(km_hi, q_hi) + _dot_nt(km_hi, q_lo) + _dot_nt(km_lo, q_hi)
        g = jnp.where(past, gt, -jnp.inf)
        sel = kblk == qblk
        for _ in range(MOBA_TOPK):
            first = jnp.min(jnp.where(g == jnp.max(g, axis=0, keepdims=True), kblk_f, float(nb)),
                            axis=0, keepdims=True)
            hit = kblk_f == first
            sel = sel | (hit & past)
            g = jnp.where(hit, -jnp.inf, g)
        bias_t = jnp.where(sel, 0.0, NEG)
        bias_t = jnp.concatenate([bias_t, jnp.zeros((LANES - nb, tm), F32)], axis=0)
        qb_ref[0, h, :, 0:LANES] = (qh * (MOBA_HD ** -0.5 * LOG2E)).astype(BF16)
        qb_ref[0, h, :, LANES:] = bias_t.T.astype(BF16)

    ql = _rms(_dot(hb, wq_ref[...]), qn_ref[...]).astype(BF16)
    scale_a = (MLA_NOPE + MLA_ROPE) ** -0.5 * LOG2E
    qn = _dot(ql, wuqn_ref[...]) * scale_a
    qr = _dot(ql, wuqr_ref[...]) * scale_a
    for h in range(MLA_HEADS):
        qa_ref[0, h, :, 0:LANES] = qn[:, h * LANES:(h + 1) * LANES].astype(BF16)
    for c in range(MLA_HEADS // 2):
        r = rope64(qr[:, c * LANES:(c + 1) * LANES])
        qa_ref[0, 2 * c, :, LANES:] = jnp.where(low_half, r, 0.0).astype(BF16)
        qa_ref[0, 2 * c + 1, :, LANES:] = jnp.where(
            low_half, pltpu.roll(r, LANES // 2, 1), 0.0).astype(BF16)

    kvl = _rms(_dot(hb, wkv_ref[...]), kvn_ref[...]).astype(BF16)
    kn = _dot(kvl, wuk_ref[...])
    vt = _dot_nt(wuv_ref[...], kvl)
    krot = rope64(_dot(hb, wpe_ref[...])).astype(BF16)
    for h in range(MLA_HEADS):
        ka_ref[0, h, :, 0:LANES] = kn[:, h * LANES:(h + 1) * LANES].astype(BF16)
        ka_ref[0, h, :, LANES:] = krot
        va_ref[0, h, 0, 0:LANES] = vt[h * LANES:(h + 1) * LANES, :].astype(BF16)
        va_ref[0, h, 0, LANES:] = ones_rows

    mvt = _dot_nt(wmv_ref[...], hb)
    for h in range(MOBA_HEADS):
        vb_ref[0, h, 0, 0:LANES] = mvt[h * LANES:(h + 1) * LANES, :].astype(BF16)
        vb_ref[0, h, 0, LANES:] = ones_rows

    z = _dot(hb, wg_ref[...]) + bg_ref[...]
    g_ref[0] = (1.0 / (1.0 + jnp.exp(-z))).astype(BF16)


def _flash_causal(nq, T, q_rows, k_tile, vt_tile, o_ref, scr):
    s_refs, x_refs, p_refs, a_refs = scr[0:2], scr[2:4], scr[4:6], scr[6:8]
    m_all, acc_all = scr[8:10]

    groups = range(T // QCOLS)
    cols = lambda c: slice(c * QCOLS, (c + 1) * QCOLS)

    def keys(diag, c):
        return (c + 1) * QCOLS if diag else T

    def scores(i, kt, slot, diag, c, ng):
        n, width = keys(diag, c), ng * QCOLS
        span = slice(c * QCOLS, c * QCOLS + width)
        s = _dot_nt(k_tile(kt, n), q_rows(i * T + c * QCOLS, width))
        if diag:
            key = lax.broadcasted_iota(jnp.int32, (n, width), 0)
            qry = lax.broadcasted_iota(jnp.int32, (n, width), 1) + c * QCOLS
            s = jnp.where(key <= qry, s, NEG)
        s_refs[slot][0:n, span] = s
        x_refs[slot][:, span] = jnp.broadcast_to(jnp.max(s, axis=0, keepdims=True),
                                                 (SUBLANES, width))

    def softmax(i, slot, diag, c):
        if diag:
            m_new = x_refs[slot][:, cols(c)]
        else:
            m_old = m_all[i, :, cols(c)]
            m_new = jnp.maximum(m_old, x_refs[slot][:, cols(c)])
            alpha = jnp.exp2(m_old - m_new)
            a_refs[slot][:, cols(c)] = alpha
        for r in range(keys(diag, c) // BF16_ROWS):
            rows = slice(r * BF16_ROWS, (r + 1) * BF16_ROWS)
            p = jnp.exp2(s_refs[slot][rows, cols(c)] - m_new[0:1, :])
            p_refs[slot][rows, cols(c)] = p.astype(BF16)
        m_all[i, :, cols(c)] = m_new

    def pv(i, kt, slot, diag, c, ng):
        n = keys(diag, c)
        span = slice(c * QCOLS, (c + ng) * QCOLS)
        r = _dot(vt_tile(kt, n), p_refs[slot][0:n, span])
        acc_all[i, :, span] = (r if diag else
                               a_refs[slot][0:1, span] * acc_all[i, :, span] + r)

    n_all = nq + nq * (nq - 1) // 2
    D, F = True, False

    def coords(n, diag):
        if diag:
            return n, n
        m = n - nq
        i = 1 + sum((m >= t * (t - 1) // 2) * 1 for t in range(2, nq))
        return i, m - (i * (i - 1)) // 2

    def block(g, ks, kx, kv, parity=None):
        xs = g % 2 if parity is None else parity
        os = 1 - xs
        ngs = 1 if ks else SCORE_GROUPS
        ngv = 1 if kv else PV_GROUPS
        for c in groups:
            if ks is not None and c % ngs == 0:
                scores(*coords(g + 1, ks), os, ks, c, ngs)
            if kx is not None:
                softmax(coords(g, kx)[0], xs, kx, c)
            if kv is not None and (c + 1) % ngv == 0:
                pv(*coords(g - 1, kv), os, kv, c + 1 - ngv, ngv)

    def steps(first, last, kind):
        def body(g, carry):
            lax.cond(g % 2 == 0, lambda: block(g, kind, kind, kind, 0),
                     lambda: block(g, kind, kind, kind, 1))
            return carry

        lax.fori_loop(first, last + 1, body, 0)

    assert nq >= 4
    block(-1, D, None, None)
    block(0, D, D, None)
    steps(1, nq - 2, D)
    block(nq - 1, F, D, D)
    block(nq, F, F, D)
    steps(nq + 1, n_all - 2, F)
    block(n_all - 1, None, F, F)
    block(n_all, None, None, F)

    def finish(i, c):
        acc = acc_all[i]
        o = acc[0:LANES] / acc[LANES:LANES + 1]
        o_ref[0, pl.ds(pl.multiple_of(i * T, T), T), :] = o.T.astype(BF16)
        return c

    lax.fori_loop(0, nq, finish, 0)


def _flash_scratch(nq, T, dv):
    return ([pltpu.VMEM((T, T), F32)] * 2 + [pltpu.VMEM((SUBLANES, T), F32)] * 2
            + [pltpu.VMEM((T, T), BF16)] * 2 + [pltpu.VMEM((SUBLANES, T), F32)] * 2
            + [pltpu.VMEM((nq, SUBLANES, T), F32), pltpu.VMEM((nq, dv, T), F32)])


def _q_rows(ref):
    return lambda start, width: ref[pl.ds(pl.multiple_of(start, QCOLS), width), :]


def _k_tile(ref, T):
    return lambda kt, n: ref[pl.ds(pl.multiple_of(kt * T, T), n), :]


def _vt_tile(ref):
    return lambda kt, n: ref[0, 0, kt, :, 0:n]


def _attn_kernel(q_ref, k_ref, vt_ref, o_ref, *scr, nq, T):
    _flash_causal(nq, T, _q_rows(q_ref.at[0, 0]), _k_tile(k_ref.at[0, 0], T),
                  _vt_tile(vt_ref), o_ref, scr)


def _ffn_kernel(x_ref, ya_ref, yb_ref, g_ref, woa_ref, wob_ref, wout_ref,
                fn_ref, wup_ref, cw_ref, cb_ref, wdn_ref, gn_ref, o_ref, carry_ref,
                act_ref, *bufs, tm, n_chunks):
    si = pl.program_id(1)

    @pl.when(si == 0)
    def _():
        carry_ref[...] = jnp.zeros_like(carry_ref)

    g = g_ref[0].astype(F32)
    mix = (g[:, :D_MODEL] * _dot(ya_ref[0], woa_ref[...])
           + g[:, D_MODEL:] * _dot(yb_ref[0], wob_ref[...]))
    x = x_ref[0] + _dot(mix.astype(BF16), wout_ref[...])
    hb = _rms(x, fn_ref[...]).astype(BF16)

    def conv(u, buf, idx, col0):
        buf[0:SUBLANES, :] = carry_ref[idx]
        buf[SUBLANES:, :] = u
        carry_ref[idx] = u[tm - SUBLANES:, :]
        u1 = buf[SUBLANES - 1:SUBLANES - 1 + tm, :]
        u2 = buf[SUBLANES - 2:SUBLANES - 2 + tm, :]
        w = cw_ref[:, col0:col0 + FF_CHUNK]
        return cb_ref[:, col0:col0 + FF_CHUNK] + w[0:1] * u2 + w[1:2] * u1 + w[2:3] * u

    for c in range(n_chunks):
        ca = c * FF_CHUNK
        cb = D_FF + c * FF_CHUNK
        ya = conv(_dot(hb, wup_ref[:, ca:ca + FF_CHUNK]), bufs[2 * (c % 2)], c, ca)
        yb = conv(_dot(hb, wup_ref[:, cb:cb + FF_CHUNK]), bufs[2 * (c % 2) + 1], n_chunks + c, cb)
        act_ref[:, ca:ca + FF_CHUNK] = (ya * (1.0 / (1.0 + jnp.exp(-ya))) * yb).astype(BF16)
    o_ref[0] = _rms(x + _dot(act_ref[...], wdn_ref[...]), gn_ref[...])


def _rope_tables(seq, d, tm):
    inv = ROPE_THETA ** (-jnp.arange(0, d, 2, dtype=F32) / d)
    lanes = lambda t: jnp.tile(jnp.concatenate([t, t], axis=-1), (1, LANES // d))
    row_ang = jnp.arange(tm).astype(F32)[:, None] * inv[None, :]
    tile_ang = (jnp.arange(seq // tm) * tm).astype(F32)[:, None] * inv[None, :]
    half = jnp.ones((1, d // 2), F32)
    sign = jnp.tile(jnp.concatenate([-half, half], axis=-1), (1, LANES // d))
    return [lanes(jnp.cos(row_ang)), lanes(jnp.sin(row_ang)),
            lanes(jnp.cos(tile_ang)), lanes(jnp.sin(tile_ang)), sign]


def _params(n_axes):
    return pltpu.CompilerParams(dimension_semantics=("arbitrary",) * n_axes,
                                vmem_limit_bytes=VMEM_LIMIT)


def _layer(x, attn_norm, w_in, b_gate, q_norm, w_uq, kv_norm, w_ukv, w_o_mla, w_o_moba,
           w_out, ffn_norm, w_up, conv_w, conv_b, w_down, out_gain):
    B, S, D = x.shape
    H = MLA_HEADS
    T = ATT_TILE
    assert D == D_MODEL and S % T == 0 and S // T >= 3
    assert D_FF % FF_CHUNK == 0 and TM_PROJ % MOBA_BLOCK == 0 and T % TM_PROJ == 0
    nq = S // T
    nb = S // MOBA_BLOCK
    assert nb <= LANES and nb % SUBLANES == 0

    wcols = lambda a, b: w_in[:, a:b].astype(BF16)
    wq = wcols(OFF_QLAT, OFF_KVLAT)
    wkv = wcols(OFF_KVLAT, OFF_KPE)
    wpe = jnp.pad(wcols(OFF_KPE, OFF_MOBA), ((0, 0), (0, LANES - MLA_ROPE)))
    hd = MOBA_HEADS * MOBA_HD
    wmq = wcols(OFF_MOBA, OFF_MOBA + hd)
    wmk = wcols(OFF_MOBA + hd, OFF_MOBA + 2 * hd)
    wmv = wcols(OFF_MOBA + 2 * hd, OFF_GATE).T
    wg = wcols(OFF_GATE, OFF_GATE + N_BRANCH * D_MODEL)
    wuq3 = w_uq.astype(BF16).reshape(Q_LORA, H, MLA_NOPE + MLA_ROPE)
    wuqn = wuq3[:, :, :MLA_NOPE].reshape(Q_LORA, H * MLA_NOPE)
    wuqr = wuq3[:, :, MLA_NOPE:].reshape(Q_LORA, H * MLA_ROPE)
    wukv3 = w_ukv.astype(BF16).reshape(KV_LORA, H, MLA_NOPE + MLA_V)
    wuk = wukv3[:, :, :MLA_NOPE].reshape(KV_LORA, H * MLA_NOPE)
    wuv = wukv3[:, :, MLA_NOPE:].reshape(KV_LORA, H * MLA_V).T
    row2 = lambda v: v.reshape(1, -1)
    tm = TM_PROJ
    ropes = _rope_tables(S, MLA_ROPE, tm) + _rope_tables(S, MOBA_HD, tm)
    head_out = lambda w, dt: jax.ShapeDtypeStruct((B, H, S, w), dt)
    head_spec = lambda w: pl.BlockSpec((1, H, tm, w), lambda b, s: (b, 0, s, 0))
    sub = T // tm
    assert MLA_V == LANES and MOBA_HD == LANES
    vt_out = lambda w: jax.ShapeDtypeStruct((B, H, nq, DV_AUG, T), BF16)
    vt_spec = lambda w: pl.BlockSpec((1, H, 1, DV_AUG, tm),
                                     lambda b, s: (b, 0, s // sub, 0, s % sub))
    weights = [row2(attn_norm), wq, wkv, wpe, wmq, wmk, wmv, wg, row2(b_gate), row2(q_norm),
               wuqn, wuqr, row2(kv_norm), wuk, wuv]
    qa, ka, va, qb, kb, vb, g = pl.pallas_call(
        _proj_kernel,
        grid=(B, S // tm),
        in_specs=[pl.BlockSpec((1, tm, D), lambda b, s: (b, s, 0))]
        + [_const_spec(w.shape) for w in weights + ropes],
        out_specs=[head_spec(ATT_QK), head_spec(ATT_QK), vt_spec(MLA_V),
                   head_spec(ATT_QK), head_spec(ATT_QK), vt_spec(MOBA_HD),
                   pl.BlockSpec((1, tm, N_BRANCH * D), lambda b, s: (b, s, 0))],
        out_shape=[head_out(ATT_QK, BF16), head_out(ATT_QK, BF16), vt_out(MLA_V),
                   head_out(ATT_QK, BF16), head_out(ATT_QK, BF16), vt_out(MOBA_HD),
                   jax.ShapeDtypeStruct((B, S, N_BRANCH * D), BF16)],
        scratch_shapes=[pltpu.VMEM((H, nb, LANES), F32)],
        compiler_params=_params(2),
        name="proj",
    )(x, *weights, *ropes)

    seq_spec = lambda w, **kw: pl.BlockSpec((1, 1, S, w), lambda b, h: (b, h, 0, 0), **kw)
    vt_in = lambda w: pl.BlockSpec((1, 1, nq, DV_AUG, T), lambda b, h: (b, h, 0, 0, 0))
    att_out = pl.BlockSpec((1, S, LANES), lambda b, h: (b, 0, h))
    def attention(name, q, k, vt):
        return pl.pallas_call(
            functools.partial(_attn_kernel, nq=nq, T=T),
            grid=(B, H),
            in_specs=[seq_spec(ATT_QK), seq_spec(ATT_QK), vt_in(LANES)],
            out_specs=att_out,
            out_shape=jax.ShapeDtypeStruct((B, S, H * LANES), BF16),
            scratch_shapes=_flash_scratch(nq, T, DV_AUG),
            compiler_params=_params(2),
            name=name,
        )(q, k, vt)

    ya = attention("mla_attn", qa, ka, va)
    yb = attention("moba_attn", qb, kb, vb)

    tm = TM_FFN
    n_chunks = D_FF // FF_CHUNK
    rows = lambda w: pl.BlockSpec((1, tm, w), lambda b, s: (b, s, 0))
    ffn_w = [w_o_mla.astype(BF16), w_o_moba.astype(BF16), w_out.astype(BF16),
             row2(ffn_norm), w_up.astype(BF16), conv_w, row2(conv_b), w_down.astype(BF16),
             row2(out_gain)]
    out = pl.pallas_call(
        functools.partial(_ffn_kernel, tm=tm, n_chunks=n_chunks),
        grid=(B, S // tm),
        in_specs=[rows(D), rows(D), rows(D), rows(N_BRANCH * D)]
        + [_const_spec(w.shape) for w in ffn_w],
        out_specs=pl.BlockSpec((1, tm, D), lambda b, s: (b, s, 0)),
        out_shape=jax.ShapeDtypeStruct((B, S, D), F32),
        scratch_shapes=[pltpu.VMEM((2 * n_chunks, SUBLANES, FF_CHUNK), F32),
                        pltpu.VMEM((tm, D_FF), BF16)]
        + [pltpu.VMEM((tm + SUBLANES, FF_CHUNK), F32)] * 4,
        compiler_params=_params(2),
        name="ffn",
    )(x, ya, yb, g, *ffn_w)
    return out


def kernel(x, attn_norm, w_in, b_gate, q_norm, w_uq, kv_norm, w_ukv, w_o_mla, w_o_moba,
           w_out, ffn_norm, w_up, conv_w, conv_b, w_down, final_norm):
    depth = w_in.shape[0]
    assert depth == 1
    return _layer(x, attn_norm[0], w_in[0], b_gate[0], q_norm[0], w_uq[0], kv_norm[0],
                  w_ukv[0], w_o_mla[0], w_o_moba[0], w_out[0], ffn_norm[0], w_up[0],
                  conv_w[0], conv_b[0], w_down[0], final_norm)
```

```python
import functools

import jax
import jax.numpy as jnp
from jax import lax
from jax.experimental import pallas as pl
from jax.experimental.pallas import tpu as pltpu

F32 = jnp.float32
BF16 = jnp.bfloat16

D_MODEL = 1024
MLA_HEADS = 8
MLA_NOPE = 128
MLA_ROPE = 64
MLA_V = 128
Q_LORA = 384
KV_LORA = 256
MOBA_HEADS = 8
MOBA_HD = 128
MOBA_BLOCK = 256
MOBA_TOPK = 3
D_FF = 2816
CONV_W = 3
ROPE_THETA = 10000.0
EPS = 1e-6
NEG = -1e30
N_BRANCH = 2
LOG2E = 1.4426950408889634

OFF_QLAT = 0
OFF_KVLAT = OFF_QLAT + Q_LORA
OFF_KPE = OFF_KVLAT + KV_LORA
OFF_MOBA = OFF_KPE + MLA_ROPE
OFF_GATE = OFF_MOBA + 3 * MOBA_HEADS * MOBA_HD

LANES = 128
SUBLANES = 8
BF16_ROWS = 16
ATT_QK = 2 * LANES
VMEM_LIMIT = 56 * 1024 * 1024

TM_PROJ = 512
ATT_TILE = 1024
QCOLS = 256
SCORE_GROUPS = 4
PV_GROUPS = 4
DV_AUG = LANES + BF16_ROWS
TM_FFN = 512
FF_CHUNK = 256


def _dot(a, b):
    return jnp.dot(a, b, preferred_element_type=F32)


def _dot_nt(a, b):
    return lax.dot_general(a, b, (((1,), (1,)), ((), ())), preferred_element_type=F32)


def _sigmoid(x):
    return 0.5 * jnp.tanh(0.5 * x) + 0.5


def _split_bf16(x):
    hi = x.astype(BF16)
    return hi, (x - hi.astype(F32)).astype(BF16)


def _rms(x, g):
    return x * lax.rsqrt(jnp.mean(x * x, axis=-1, keepdims=True) + EPS) * g


def _const_spec(shape):
    n = len(shape)
    return pl.BlockSpec(shape, lambda *_: (0,) * n, pipeline_mode=pl.Buffered(1))


def _proj_kernel(x_ref, an_ref, wq_ref, wkv_ref, wpe_ref, wmq_ref, wmk_ref, wmv_ref, wg_ref,
                 bg_ref, qn_ref, wuqn_ref, wuqr_ref, kvn_ref, wuk_ref, wuv_ref,
                 cra_ref, sra_ref, cta_ref, sta_ref, sga_ref,
                 crb_ref, srb_ref, ctb_ref, stb_ref, sgb_ref,
                 qa_ref, ka_ref, va_ref, qb_ref, kb_ref, vb_ref, g_ref, kmean_ref):
    hb = _rms(x_ref[0], an_ref[...]).astype(BF16)
    lane = lax.broadcasted_iota(jnp.int32, (hb.shape[0], LANES), 1)
    low_half = lane < MLA_ROPE
    first32 = (lane % MLA_ROPE) < (MLA_ROPE // 2)

    def rope_table(refs):
        cr_ref, sr_ref, ct_ref, st_ref, sign_ref = refs
        tile = pl.ds(pl.program_id(1), 1)
        cr, sr, ct, st = cr_ref[...], sr_ref[...], ct_ref[tile, :], st_ref[tile, :]
        return ct * cr - st * sr, (st * cr + ct * sr) * sign_ref[...]

    cosa, sina = rope_table((cra_ref, sra_ref, cta_ref, sta_ref, sga_ref))
    cosb, sinb = rope_table((crb_ref, srb_ref, ctb_ref, stb_ref, sgb_ref))
    ones_rows = jnp.ones((DV_AUG - LANES, hb.shape[0]), BF16)

    def rope64(c):
        half = MLA_ROPE // 2
        partner = jnp.where(first32, pltpu.roll(c, LANES - half, 1), pltpu.roll(c, half, 1))
        return c * cosa + partner * sina

    def rope128(c):
        return c * cosb + pltpu.roll(c, LANES // 2, 1) * sinb

    tm = hb.shape[0]
    nb = kmean_ref.shape[1]
    blocks_per_tile = tm // MOBA_BLOCK
    si = pl.program_id(1)

    @pl.when(si == 0)
    def _():
        kmean_ref[...] = jnp.zeros_like(kmean_ref)

    row_blk = lax.broadcasted_iota(jnp.int32, (tm, LANES), 0) // MOBA_BLOCK
    onehot = jnp.where(lane == si * blocks_per_tile + row_blk, 1.0, 0.0).astype(BF16)
    mk = _dot(hb, wmk_ref[...])
    for h in range(MOBA_HEADS):
        kh = rope128(mk[:, h * LANES:(h + 1) * LANES])
        kb_ref[0, h, :, 0:LANES] = kh.astype(BF16)
        kb_ref[0, h, :, LANES:] = onehot
        kmean_ref[h, pl.ds(si * blocks_per_tile, blocks_per_tile), :] = jnp.mean(
            kh.reshape(blocks_per_tile, MOBA_BLOCK, LANES), axis=1)

    kblk = lax.broadcasted_iota(jnp.int32, (nb, tm), 0)
    kblk_f = kblk.astype(F32)
    qblk = si * blocks_per_tile + lax.broadcasted_iota(jnp.int32, (nb, tm), 1) // MOBA_BLOCK
    past = kblk < qblk
    mq = _dot(hb, wmq_ref[...])
    for h in range(MOBA_HEADS):
        qh = rope128(mq[:, h * LANES:(h + 1) * LANES])
        km_hi, km_lo = _split_bf16(kmean_ref[h])
        q_hi, q_lo = _split_bf16(qh)
        gt = _dot_nt(km_hi, q_hi) + _dot_nt(km_hi, q_lo) + _dot_nt(km_lo, q_hi)
        g = jnp.where(past, gt, -jnp.inf)
        sel = kblk == qblk
        for _ in range(MOBA_TOPK):
            first = jnp.min(jnp.where(g == jnp.max(g, axis=0, keepdims=True), kblk_f, float(nb)),
                            axis=0, keepdims=True)
            hit = kblk_f == first
            sel = sel | (hit & past)
            g = jnp.where(hit, -jnp.inf, g)
        bias_t = jnp.where(sel, 0.0, NEG)
        bias_t = jnp.concatenate([bias_t, jnp.zeros((LANES - nb, tm), F32)], axis=0)
        qb_ref[0, h, :, 0:LANES] = (qh * (MOBA_HD ** -0.5 * LOG2E)).astype(BF16)
        qb_ref[0, h, :, LANES:] = bias_t.T.astype(BF16)

    ql = _rms(_dot(hb, wq_ref[...]), qn_ref[...]).astype(BF16)
    scale_a = (MLA_NOPE + MLA_ROPE) ** -0.5 * LOG2E
    qn = _dot(ql, wuqn_ref[...]) * scale_a
    qr = _dot(ql, wuqr_ref[...]) * scale_a
    for h in range(MLA_HEADS):
        qa_ref[0, h, :, 0:LANES] = qn[:, h * LANES:(h + 1) * LANES].astype(BF16)
    for c in range(MLA_HEADS // 2):
        r = rope64(qr[:, c * LANES:(c + 1) * LANES])
        qa_ref[0, 2 * c, :, LANES:] = jnp.where(low_half, r, 0.0).astype(BF16)
        qa_ref[0, 2 * c + 1, :, LANES:] = jnp.where(
            low_half, pltpu.roll(r, LANES // 2, 1), 0.0).astype(BF16)

    kvl = _rms(_dot(hb, wkv_ref[...]), kvn_ref[...]).astype(BF16)
    kn = _dot(kvl, wuk_ref[...])
    vt = _dot_nt(wuv_ref[...], kvl)
    krot = rope64(_dot(hb, wpe_ref[...])).astype(BF16)
    for h in range(MLA_HEADS):
        ka_ref[0, h, :, 0:LANES] = kn[:, h * LANES:(h + 1) * LANES].astype(BF16)
        ka_ref[0, h, :, LANES:] = krot
        va_ref[0, h, 0, 0:LANES] = vt[h * LANES:(h + 1) * LANES, :].astype(BF16)
        va_ref[0, h, 0, LANES:] = ones_rows

    mvt = _dot_nt(wmv_ref[...], hb)
    for h in range(MOBA_HEADS):
        vb_ref[0, h, 0, 0:LANES] = mvt[h * LANES:(h + 1) * LANES, :].astype(BF16)
        vb_ref[0, h, 0, LANES:] = ones_rows

    z = _dot(hb, wg_ref[...]) + bg_ref[...]
    g_ref[0] = _sigmoid(z).astype(BF16)


def _flash_causal(nq, T, q_rows, k_tile, vt_tile, o_ref, scr):
    s_refs, x_refs, p_refs, a_refs = scr[0:2], scr[2:4], scr[4:6], scr[6:8]
    m_all, acc_all = scr[8:10]

    groups = range(T // QCOLS)
    cols = lambda c: slice(c * QCOLS, (c + 1) * QCOLS)

    def keys(diag, c):
        return (c + 1) * QCOLS if diag else T

    def scores(i, kt, slot, diag, c, ng):
        n, width = keys(diag, c), ng * QCOLS
        span = slice(c * QCOLS, c * QCOLS + width)
        s = _dot_nt(k_tile(kt, n), q_rows(i * T + c * QCOLS, width))
        if diag:
            key = lax.broadcasted_iota(jnp.int32, (n, width), 0)
            qry = lax.broadcasted_iota(jnp.int32, (n, width), 1) + c * QCOLS
            s = jnp.where(key <= qry, s, NEG)
        s_refs[slot][0:n, span] = s
        x_refs[slot][:, span] = jnp.broadcast_to(jnp.max(s, axis=0, keepdims=True),
                                                 (SUBLANES, width))

    def softmax(i, slot, diag, c):
        if diag:
            m_new = x_refs[slot][:, cols(c)]
        else:
            m_old = m_all[i, :, cols(c)]
            m_new = jnp.maximum(m_old, x_refs[slot][:, cols(c)])
            alpha = jnp.exp2(m_old - m_new)
            a_refs[slot][:, cols(c)] = alpha
        for r in range(keys(diag, c) // BF16_ROWS):
            rows = slice(r * BF16_ROWS, (r + 1) * BF16_ROWS)
            p = jnp.exp2(s_refs[slot][rows, cols(c)] - m_new[0:1, :])
            p_refs[slot][rows, cols(c)] = p.astype(BF16)
        m_all[i, :, cols(c)] = m_new

    def pv(i, kt, slot, diag, c, ng):
        n = keys(diag, c)
        span = slice(c * QCOLS, (c + ng) * QCOLS)
        r = _dot(vt_tile(kt, n), p_refs[slot][0:n, span])
        acc_all[i, :, span] = (r if diag else
                               a_refs[slot][0:1, span] * acc_all[i, :, span] + r)

    n_all = nq + nq * (nq - 1) // 2
    D, F = True, False

    def coords(n, diag):
        if diag:
            return n, n
        m = n - nq
        i = 1 + sum((m >= t * (t - 1) // 2) * 1 for t in range(2, nq))
        return i, m - (i * (i - 1)) // 2

    def block(g, ks, kx, kv, parity=None):
        xs = g % 2 if parity is None else parity
        os = 1 - xs
        ngs = 1 if ks else SCORE_GROUPS
        ngv = 1 if kv else PV_GROUPS
        for c in groups:
            if ks is not None and c % ngs == 0:
                scores(*coords(g + 1, ks), os, ks, c, ngs)
            if kx is not None:
                softmax(coords(g, kx)[0], xs, kx, c)
            if kv is not None and (c + 1) % ngv == 0:
                pv(*coords(g - 1, kv), os, kv, c + 1 - ngv, ngv)

    def steps(first, last, kind):
        def body(g, carry):
            lax.cond(g % 2 == 0, lambda: block(g, kind, kind, kind, 0),
                     lambda: block(g, kind, kind, kind, 1))
            return carry

        lax.fori_loop(first, last + 1, body, 0)

    assert nq >= 4
    block(-1, D, None, None)
    block(0, D, D, None)
    steps(1, nq - 2, D)
    block(nq - 1, F, D, D)
    block(nq, F, F, D)
    steps(nq + 1, n_all - 2, F)
    block(n_all - 1, None, F, F)
    block(n_all, None, None, F)

    def finish(i, c):
        acc = acc_all[i]
        o = acc[0:LANES] / acc[LANES:LANES + 1]
        o_ref[0, pl.ds(pl.multiple_of(i * T, T), T), :] = o.T.astype(BF16)
        return c

    lax.fori_loop(0, nq, finish, 0)


def _flash_scratch(nq, T, dv):
    return ([pltpu.VMEM((T, T), F32)] * 2 + [pltpu.VMEM((SUBLANES, T), F32)] * 2
            + [pltpu.VMEM((T, T), BF16)] * 2 + [pltpu.VMEM((SUBLANES, T), F32)] * 2
            + [pltpu.VMEM((nq, SUBLANES, T), F32), pltpu.VMEM((nq, dv, T), F32)])


def _q_rows(ref):
    return lambda start, width: ref[pl.ds(pl.multiple_of(start, QCOLS), width), :]


def _k_tile(ref, T):
    return lambda kt, n: ref[pl.ds(pl.multiple_of(kt * T, T), n), :]


def _vt_tile(ref):
    return lambda kt, n: ref[0, 0, kt, :, 0:n]


def _attn_kernel(q_ref, k_ref, vt_ref, o_ref, *scr, nq, T):
    _flash_causal(nq, T, _q_rows(q_ref.at[0, 0]), _k_tile(k_ref.at[0, 0], T),
                  _vt_tile(vt_ref), o_ref, scr)


def _ffn_kernel(x_ref, ya_ref, yb_ref, g_ref, woa_ref, wob_ref, wout_ref,
                fn_ref, wup_ref, cw_ref, cb_ref, wdn_ref, gn_ref, o_ref, carry_ref,
                act_ref, *bufs, tm, n_chunks):
    si = pl.program_id(1)

    @pl.when(si == 0)
    def _():
        carry_ref[...] = jnp.zeros_like(carry_ref)

    g = g_ref[0].astype(F32)
    mix = (g[:, :D_MODEL] * _dot(ya_ref[0], woa_ref[...])
           + g[:, D_MODEL:] * _dot(yb_ref[0], wob_ref[...]))
    x = x_ref[0] + _dot(mix.astype(BF16), wout_ref[...])
    hb = _rms(x, fn_ref[...]).astype(BF16)

    def conv(u, buf, idx, col0):
        buf[0:SUBLANES, :] = carry_ref[idx]
        buf[SUBLANES:, :] = u
        carry_ref[idx] = u[tm - SUBLANES:, :]
        u1 = buf[SUBLANES - 1:SUBLANES - 1 + tm, :]
        u2 = buf[SUBLANES - 2:SUBLANES - 2 + tm, :]
        w = cw_ref[:, col0:col0 + FF_CHUNK]
        return cb_ref[:, col0:col0 + FF_CHUNK] + w[0:1] * u2 + w[1:2] * u1 + w[2:3] * u

    for c in range(n_chunks):
        ca = c * FF_CHUNK
        cb = D_FF + c * FF_CHUNK
        ya = conv(_dot(hb, wup_ref[:, ca:ca + FF_CHUNK]), bufs[2 * (c % 2)], c, ca)
        yb = conv(_dot(hb, wup_ref[:, cb:cb + FF_CHUNK]), bufs[2 * (c % 2) + 1], n_chunks + c, cb)
        act_ref[:, ca:ca + FF_CHUNK] = (ya * _sigmoid(ya) * yb).astype(BF16)
    o_ref[0] = _rms(x + _dot(act_ref[...], wdn_ref[...]), gn_ref[...])


def _rope_tables(seq, d, tm):
    inv = ROPE_THETA ** (-jnp.arange(0, d, 2, dtype=F32) / d)
    lanes = lambda t: jnp.tile(jnp.concatenate([t, t], axis=-1), (1, LANES // d))
    row_ang = jnp.arange(tm).astype(F32)[:, None] * inv[None, :]
    tile_ang = (jnp.arange(seq // tm) * tm).astype(F32)[:, None] * inv[None, :]
    half = jnp.ones((1, d // 2), F32)
    sign = jnp.tile(jnp.concatenate([-half, half], axis=-1), (1, LANES // d))
    return [lanes(jnp.cos(row_ang)), lanes(jnp.sin(row_ang)),
            lanes(jnp.cos(tile_ang)), lanes(jnp.sin(tile_ang)), sign]


def _params(n_axes):
    return pltpu.CompilerParams(dimension_semantics=("arbitrary",) * n_axes,
                                vmem_limit_bytes=VMEM_LIMIT)


def _layer(x, attn_norm, w_in, b_gate, q_norm, w_uq, kv_norm, w_ukv, w_o_mla, w_o_moba,
           w_out, ffn_norm, w_up, conv_w, conv_b, w_down, out_gain):
    B, S, D = x.shape
    H = MLA_HEADS
    T = ATT_TILE
    assert D == D_MODEL and S % T == 0 and S // T >= 3
    assert D_FF % FF_CHUNK == 0 and TM_PROJ % MOBA_BLOCK == 0 and T % TM_PROJ == 0
    nq = S // T
    nb = S // MOBA_BLOCK
    assert nb <= LANES and nb % SUBLANES == 0

    wcols = lambda a, b: w_in[:, a:b].astype(BF16)
    wq = wcols(OFF_QLAT, OFF_KVLAT)
    wkv = wcols(OFF_KVLAT, OFF_KPE)
    wpe = jnp.pad(wcols(OFF_KPE, OFF_MOBA), ((0, 0), (0, LANES - MLA_ROPE)))
    hd = MOBA_HEADS * MOBA_HD
    wmq = wcols(OFF_MOBA, OFF_MOBA + hd)
    wmk = wcols(OFF_MOBA + hd, OFF_MOBA + 2 * hd)
    wmv = wcols(OFF_MOBA + 2 * hd, OFF_GATE).T
    wg = wcols(OFF_GATE, OFF_GATE + N_BRANCH * D_MODEL)
    wuq3 = w_uq.astype(BF16).reshape(Q_LORA, H, MLA_NOPE + MLA_ROPE)
    wuqn = wuq3[:, :, :MLA_NOPE].reshape(Q_LORA, H * MLA_NOPE)
    wuqr = wuq3[:, :, MLA_NOPE:].reshape(Q_LORA, H * MLA_ROPE)
    wukv3 = w_ukv.astype(BF16).reshape(KV_LORA, H, MLA_NOPE + MLA_V)
    wuk = wukv3[:, :, :MLA_NOPE].reshape(KV_LORA, H * MLA_NOPE)
    wuv = wukv3[:, :, MLA_NOPE:].reshape(KV_LORA, H * MLA_V).T
    row2 = lambda v: v.reshape(1, -1)
    tm = TM_PROJ
    ropes = _rope_tables(S, MLA_ROPE, tm) + _rope_tables(S, MOBA_HD, tm)
    head_out = lambda w, dt: jax.ShapeDtypeStruct((B, H, S, w), dt)
    head_spec = lambda w: pl.BlockSpec((1, H, tm, w), lambda b, s: (b, 0, s, 0))
    sub = T // tm
    assert MLA_V == LANES and MOBA_HD == LANES
    vt_out = lambda w: jax.ShapeDtypeStruct((B, H, nq, DV_AUG, T), BF16)
    vt_spec = lambda w: pl.BlockSpec((1, H, 1, DV_AUG, tm),
                                     lambda b, s: (b, 0, s // sub, 0, s % sub))
    weights = [row2(attn_norm), wq, wkv, wpe, wmq, wmk, wmv, wg, row2(b_gate), row2(q_norm),
               wuqn, wuqr, row2(kv_norm), wuk, wuv]
    qa, ka, va, qb, kb, vb, g = pl.pallas_call(
        _proj_kernel,
        grid=(B, S // tm),
        in_specs=[pl.BlockSpec((1, tm, D), lambda b, s: (b, s, 0))]
        + [_const_spec(w.shape) for w in weights + ropes],
        out_specs=[head_spec(ATT_QK), head_spec(ATT_QK), vt_spec(MLA_V),
                   head_spec(ATT_QK), head_spec(ATT_QK), vt_spec(MOBA_HD),
                   pl.BlockSpec((1, tm, N_BRANCH * D), lambda b, s: (b, s, 0))],
        out_shape=[head_out(ATT_QK, BF16), head_out(ATT_QK, BF16), vt_out(MLA_V),
                   head_out(ATT_QK, BF16), head_out(ATT_QK, BF16), vt_out(MOBA_HD),
                   jax.ShapeDtypeStruct((B, S, N_BRANCH * D), BF16)],
        scratch_shapes=[pltpu.VMEM((H, nb, LANES), F32)],
        compiler_params=_params(2),
        name="proj",
    )(x, *weights, *ropes)

    seq_spec = lambda w, **kw: pl.BlockSpec((1, 1, S, w), lambda b, h: (b, h, 0, 0), **kw)
    vt_in = lambda w: pl.BlockSpec((1, 1, nq, DV_AUG, T), lambda b, h: (b, h, 0, 0, 0))
    att_out = pl.BlockSpec((1, S, LANES), lambda b, h: (b, 0, h))
    def attention(name, q, k, vt):
        return pl.pallas_call(
            functools.partial(_attn_kernel, nq=nq, T=T),
            grid=(B, H),
            in_specs=[seq_spec(ATT_QK), seq_spec(ATT_QK), vt_in(LANES)],
            out_specs=att_out,
            out_shape=jax.ShapeDtypeStruct((B, S, H * LANES), BF16),
            scratch_shapes=_flash_scratch(nq, T, DV_AUG),
            compiler_params=_params(2),
            name=name,
        )(q, k, vt)

    ya = attention("mla_attn", qa, ka, va)
    yb = attention("moba_attn", qb, kb, vb)

    tm = TM_FFN
    n_chunks = D_FF // FF_CHUNK
    rows = lambda w: pl.BlockSpec((1, tm, w), lambda b, s: (b, s, 0))
    ffn_w = [w_o_mla.astype(BF16), w_o_moba.astype(BF16), w_out.astype(BF16),
             row2(ffn_norm), w_up.astype(BF16), conv_w, row2(conv_b), w_down.astype(BF16),
             row2(out_gain)]
    out = pl.pallas_call(
        functools.partial(_ffn_kernel, tm=tm, n_chunks=n_chunks),
        grid=(B, S // tm),
        in_specs=[rows(D), rows(D), rows(D), rows(N_BRANCH * D)]
        + [_const_spec(w.shape) for w in ffn_w],
        out_specs=pl.BlockSpec((1, tm, D), lambda b, s: (b, s, 0)),
        out_shape=jax.ShapeDtypeStruct((B, S, D), F32),
        scratch_shapes=[pltpu.VMEM((2 * n_chunks, SUBLANES, FF_CHUNK), F32),
                        pltpu.VMEM((tm, D_FF), BF16)]
        + [pltpu.VMEM((tm + SUBLANES, FF_CHUNK), F32)] * 4,
        compiler_params=_params(2),
        name="ffn",
    )(x, ya, yb, g, *ffn_w)
    return out


def kernel(x, attn_norm, w_in, b_gate, q_norm, w_uq, kv_norm, w_ukv, w_o_mla, w_o_moba,
           w_out, ffn_norm, w_up, conv_w, conv_b, w_down, final_norm):
    depth = w_in.shape[0]
    assert depth == 1
    return _layer(x, attn_norm[0], w_in[0], b_gate[0], q_norm[0], w_uq[0], kv_norm[0],
                  w_ukv[0], w_o_mla[0], w_o_moba[0], w_out[0], ffn_norm[0], w_up[0],
                  conv_w[0], conv_b[0], w_down[0], final_norm)
```

```python
import functools

import jax
import jax.numpy as jnp
from jax import lax
from jax.experimental import pallas as pl
from jax.experimental.pallas import tpu as pltpu

F32 = jnp.float32
BF16 = jnp.bfloat16

D_MODEL = 1024
MLA_HEADS = 8
MLA_NOPE = 128
MLA_ROPE = 64
MLA_V = 128
Q_LORA = 384
KV_LORA = 256
MOBA_HEADS = 8
MOBA_HD = 128
MOBA_BLOCK = 256
MOBA_TOPK = 3
D_FF = 2816
CONV_W = 3
ROPE_THETA = 10000.0
EPS = 1e-6
NEG = -1e30
N_BRANCH = 2
LOG2E = 1.4426950408889634

OFF_QLAT = 0
OFF_KVLAT = OFF_QLAT + Q_LORA
OFF_KPE = OFF_KVLAT + KV_LORA
OFF_MOBA = OFF_KPE + MLA_ROPE
OFF_GATE = OFF_MOBA + 3 * MOBA_HEADS * MOBA_HD

LANES = 128
SUBLANES = 8
BF16_ROWS = 16
ATT_QK = 2 * LANES
VMEM_LIMIT = 56 * 1024 * 1024

TM_PROJ = 512
ATT_TILE = 1024
QCOLS = 256
SCORE_GROUPS = 4
PV_GROUPS = 4
DV_AUG = LANES + BF16_ROWS
TM_FFN = 512
FF_CHUNK = 256


def _dot(a, b):
    return jnp.dot(a, b, preferred_element_type=F32)


def _dot_nt(a, b):
    return lax.dot_general(a, b, (((1,), (1,)), ((), ())), preferred_element_type=F32)


def _sigmoid(x):
    return 0.5 * jnp.tanh(0.5 * x) + 0.5


def _split_bf16(x):
    hi = x.astype(BF16)
    return hi, (x - hi.astype(F32)).astype(BF16)


def _rms(x, g):
    return x * lax.rsqrt(jnp.mean(x * x, axis=-1, keepdims=True) + EPS) * g


def _const_spec(shape):
    n = len(shape)
    return pl.BlockSpec(shape, lambda *_: (0,) * n, pipeline_mode=pl.Buffered(1))


def _proj_kernel(x_ref, an_ref, wq_ref, wkv_ref, wpe_ref, wmq_ref, wmk_ref, wmv_ref, wg_ref,
                 bg_ref, qn_ref, wuqn_ref, wuqr_ref, kvn_ref, wuk_ref, wuv_ref,
                 cra_ref, sra_ref, cta_ref, sta_ref, sga_ref,
                 crb_ref, srb_ref, ctb_ref, stb_ref, sgb_ref,
                 q_ref, k_ref, v_ref, g_ref, kmean_ref):
    nh = MLA_HEADS
    qa_ref, qb_ref = q_ref.at[:, 0:nh], q_ref.at[:, nh:2 * nh]
    ka_ref, kb_ref = k_ref.at[:, 0:nh], k_ref.at[:, nh:2 * nh]
    va_ref, vb_ref = v_ref.at[:, 0:nh], v_ref.at[:, nh:2 * nh]
    hb = _rms(x_ref[0], an_ref[...]).astype(BF16)
    lane = lax.broadcasted_iota(jnp.int32, (hb.shape[0], LANES), 1)
    low_half = lane < MLA_ROPE
    first32 = (lane % MLA_ROPE) < (MLA_ROPE // 2)

    def rope_table(refs):
        cr_ref, sr_ref, ct_ref, st_ref, sign_ref = refs
        tile = pl.ds(pl.program_id(1), 1)
        cr, sr, ct, st = cr_ref[...], sr_ref[...], ct_ref[tile, :], st_ref[tile, :]
        return ct * cr - st * sr, (st * cr + ct * sr) * sign_ref[...]

    cosa, sina = rope_table((cra_ref, sra_ref, cta_ref, sta_ref, sga_ref))
    cosb, sinb = rope_table((crb_ref, srb_ref, ctb_ref, stb_ref, sgb_ref))
    ones_rows = jnp.ones((DV_AUG - LANES, hb.shape[0]), BF16)

    def rope64(c):
        half = MLA_ROPE // 2
        partner = jnp.where(first32, pltpu.roll(c, LANES - half, 1), pltpu.roll(c, half, 1))
        return c * cosa + partner * sina

    def rope128(c):
        return c * cosb + pltpu.roll(c, LANES // 2, 1) * sinb

    tm = hb.shape[0]
    nb = kmean_ref.shape[1]
    blocks_per_tile = tm // MOBA_BLOCK
    si = pl.program_id(1)

    @pl.when(si == 0)
    def _():
        kmean_ref[...] = jnp.zeros_like(kmean_ref)

    row_blk = lax.broadcasted_iota(jnp.int32, (tm, LANES), 0) // MOBA_BLOCK
    onehot = jnp.where(lane == si * blocks_per_tile + row_blk, 1.0, 0.0).astype(BF16)
    mk = _dot(hb, wmk_ref[...])
    for h in range(MOBA_HEADS):
        kh = rope128(mk[:, h * LANES:(h + 1) * LANES])
        kb_ref[0, h, :, 0:LANES] = kh.astype(BF16)
        kb_ref[0, h, :, LANES:] = onehot
        kmean_ref[h, pl.ds(si * blocks_per_tile, blocks_per_tile), :] = jnp.mean(
            kh.reshape(blocks_per_tile, MOBA_BLOCK, LANES), axis=1)

    kblk = lax.broadcasted_iota(jnp.int32, (nb, tm), 0)
    kblk_f = kblk.astype(F32)
    qblk = si * blocks_per_tile + lax.broadcasted_iota(jnp.int32, (nb, tm), 1) // MOBA_BLOCK
    past = kblk < qblk
    mq = _dot(hb, wmq_ref[...])
    for h in range(MOBA_HEADS):
        qh = rope128(mq[:, h * LANES:(h + 1) * LANES])
        km_hi, km_lo = _split_bf16(kmean_ref[h])
        q_hi, q_lo = _split_bf16(qh)
        gt = _dot_nt(km_hi, q_hi) + _dot_nt(km_hi, q_lo) + _dot_nt(km_lo, q_hi)
        g = jnp.where(past, gt, -jnp.inf)
        sel = kblk == qblk
        for _ in range(MOBA_TOPK):
            first = jnp.min(jnp.where(g == jnp.max(g, axis=0, keepdims=True), kblk_f, float(nb)),
                            axis=0, keepdims=True)
            hit = kblk_f == first
            sel = sel | (hit & past)
            g = jnp.where(hit, -jnp.inf, g)
        bias_t = jnp.where(sel, 0.0, NEG)
        bias_t = jnp.concatenate([bias_t, jnp.zeros((LANES - nb, tm), F32)], axis=0)
        qb_ref[0, h, :, 0:LANES] = (qh * (MOBA_HD ** -0.5 * LOG2E)).astype(BF16)
        qb_ref[0, h, :, LANES:] = bias_t.T.astype(BF16)

    ql = _rms(_dot(hb, wq_ref[...]), qn_ref[...]).astype(BF16)
    scale_a = (MLA_NOPE + MLA_ROPE) ** -0.5 * LOG2E
    qn = _dot(ql, wuqn_ref[...]) * scale_a
    qr = _dot(ql, wuqr_ref[...]) * scale_a
    for h in range(MLA_HEADS):
        qa_ref[0, h, :, 0:LANES] = qn[:, h * LANES:(h + 1) * LANES].astype(BF16)
    for c in range(MLA_HEADS // 2):
        r = rope64(qr[:, c * LANES:(c + 1) * LANES])
        qa_ref[0, 2 * c, :, LANES:] = jnp.where(low_half, r, 0.0).astype(BF16)
        qa_ref[0, 2 * c + 1, :, LANES:] = jnp.where(
            low_half, pltpu.roll(r, LANES // 2, 1), 0.0).astype(BF16)

    kvl = _rms(_dot(hb, wkv_ref[...]), kvn_ref[...]).astype(BF16)
    kn = _dot(kvl, wuk_ref[...])
    vt = _dot_nt(wuv_ref[...], kvl)
    krot = rope64(_dot(hb, wpe_ref[...])).astype(BF16)
    for h in range(MLA_HEADS):
        ka_ref[0, h, :, 0:LANES] = kn[:, h * LANES:(h + 1) * LANES].astype(BF16)
        ka_ref[0, h, :, LANES:] = krot
        va_ref[0, h, 0, 0:LANES] = vt[h * LANES:(h + 1) * LANES, :].astype(BF16)
        va_ref[0, h, 0, LANES:] = ones_rows

    mvt = _dot_nt(wmv_ref[...], hb)
    for h in range(MOBA_HEADS):
        vb_ref[0, h, 0, 0:LANES] = mvt[h * LANES:(h + 1) * LANES, :].astype(BF16)
        vb_ref[0, h, 0, LANES:] = ones_rows

    z = _dot(hb, wg_ref[...]) + bg_ref[...]
    g_ref[0] = _sigmoid(z).astype(BF16)


def _flash_causal(nq, T, q_rows, k_tile, vt_tile, o_ref, scr):
    s_refs, x_refs, p_refs, a_refs = scr[0:2], scr[2:4], scr[4:6], scr[6:8]
    m_all, acc_all = scr[8:10]

    groups = range(T // QCOLS)
    cols = lambda c: slice(c * QCOLS, (c + 1) * QCOLS)

    def keys(diag, c):
        return (c + 1) * QCOLS if diag else T

    def scores(i, kt, slot, diag, c, ng):
        n, width = keys(diag, c), ng * QCOLS
        span = slice(c * QCOLS, c * QCOLS + width)
        s = _dot_nt(k_tile(kt, n), q_rows(i * T + c * QCOLS, width))
        if diag:
            key = lax.broadcasted_iota(jnp.int32, (n, width), 0)
            qry = lax.broadcasted_iota(jnp.int32, (n, width), 1) + c * QCOLS
            s = jnp.where(key <= qry, s, NEG)
        s_refs[slot][0:n, span] = s
        x_refs[slot][:, span] = jnp.broadcast_to(jnp.max(s, axis=0, keepdims=True),
                                                 (SUBLANES, width))

    def softmax(i, slot, diag, c):
        if diag:
            m_new = x_refs[slot][:, cols(c)]
        else:
            m_old = m_all[i, :, cols(c)]
            m_new = jnp.maximum(m_old, x_refs[slot][:, cols(c)])
            alpha = jnp.exp2(m_old - m_new)
            a_refs[slot][:, cols(c)] = alpha
        for r in range(keys(diag, c) // BF16_ROWS):
            rows = slice(r * BF16_ROWS, (r + 1) * BF16_ROWS)
            p = jnp.exp2(s_refs[slot][rows, cols(c)] - m_new[0:1, :])
            p_refs[slot][rows, cols(c)] = p.astype(BF16)
        m_all[i, :, cols(c)] = m_new

    def pv(i, kt, slot, diag, c, ng):
        n = keys(diag, c)
        span = slice(c * QCOLS, (c + ng) * QCOLS)
        r = _dot(vt_tile(kt, n), p_refs[slot][0:n, span])
        acc_all[i, :, span] = (r if diag else
                               a_refs[slot][0:1, span] * acc_all[i, :, span] + r)

    n_all = nq + nq * (nq - 1) // 2
    D, F = True, False

    def coords(n, diag):
        if diag:
            return n, n
        m = n - nq
        i = 1 + sum((m >= t * (t - 1) // 2) * 1 for t in range(2, nq))
        return i, m - (i * (i - 1)) // 2

    def block(g, ks, kx, kv, parity=None):
        xs = g % 2 if parity is None else parity
        os = 1 - xs
        ngs = 1 if ks else SCORE_GROUPS
        ngv = 1 if kv else PV_GROUPS
        for c in groups:
            if ks is not None and c % ngs == 0:
                scores(*coords(g + 1, ks), os, ks, c, ngs)
            if kx is not None:
                softmax(coords(g, kx)[0], xs, kx, c)
            if kv is not None and (c + 1) % ngv == 0:
                pv(*coords(g - 1, kv), os, kv, c + 1 - ngv, ngv)

    def steps(first, last, kind):
        def body(g, carry):
            lax.cond(g % 2 == 0, lambda: block(g, kind, kind, kind, 0),
                     lambda: block(g, kind, kind, kind, 1))
            return carry

        lax.fori_loop(first, last + 1, body, 0)

    assert nq >= 4
    block(-1, D, None, None)
    block(0, D, D, None)
    steps(1, nq - 2, D)
    block(nq - 1, F, D, D)
    block(nq, F, F, D)
    steps(nq + 1, n_all - 2, F)
    block(n_all - 1, None, F, F)
    block(n_all, None, None, F)

    def finish(i, c):
        acc = acc_all[i]
        o = acc[0:LANES] / acc[LANES:LANES + 1]
        o_ref[0, pl.ds(pl.multiple_of(i * T, T), T), :] = o.T.astype(BF16)
        return c

    lax.fori_loop(0, nq, finish, 0)


def _flash_scratch(nq, T, dv):
    return ([pltpu.VMEM((T, T), F32)] * 2 + [pltpu.VMEM((SUBLANES, T), F32)] * 2
            + [pltpu.VMEM((T, T), BF16)] * 2 + [pltpu.VMEM((SUBLANES, T), F32)] * 2
            + [pltpu.VMEM((nq, SUBLANES, T), F32), pltpu.VMEM((nq, dv, T), F32)])


def _q_rows(ref):
    return lambda start, width: ref[pl.ds(pl.multiple_of(start, QCOLS), width), :]


def _k_tile(ref, T):
    return lambda kt, n: ref[pl.ds(pl.multiple_of(kt * T, T), n), :]


def _vt_tile(ref):
    return lambda kt, n: ref[0, 0, kt, :, 0:n]


def _attn_kernel(q_ref, k_ref, vt_ref, o_ref, *scr, nq, T):
    _flash_causal(nq, T, _q_rows(q_ref.at[0, 0]), _k_tile(k_ref.at[0, 0], T),
                  _vt_tile(vt_ref), o_ref, scr)


def _ffn_kernel(x_ref, ya_ref, yb_ref, g_ref, woa_ref, wob_ref, wout_ref,
                fn_ref, wup_ref, cw_ref, cb_ref, wdn_ref, gn_ref, o_ref, carry_ref,
                act_ref, *bufs, tm, n_chunks):
    si = pl.program_id(1)

    @pl.when(si == 0)
    def _():
        carry_ref[...] = jnp.zeros_like(carry_ref)

    g = g_ref[0].astype(F32)
    mix = (g[:, :D_MODEL] * _dot(ya_ref[0], woa_ref[...])
           + g[:, D_MODEL:] * _dot(yb_ref[0], wob_ref[...]))
    x = x_ref[0] + _dot(mix.astype(BF16), wout_ref[...])
    hb = _rms(x, fn_ref[...]).astype(BF16)

    def conv(u, buf, idx, col0):
        buf[0:SUBLANES, :] = carry_ref[idx]
        buf[SUBLANES:, :] = u
        carry_ref[idx] = u[tm - SUBLANES:, :]
        u1 = buf[SUBLANES - 1:SUBLANES - 1 + tm, :]
        u2 = buf[SUBLANES - 2:SUBLANES - 2 + tm, :]
        w = cw_ref[:, col0:col0 + FF_CHUNK]
        return cb_ref[:, col0:col0 + FF_CHUNK] + w[0:1] * u2 + w[1:2] * u1 + w[2:3] * u

    for c in range(n_chunks):
        ca = c * FF_CHUNK
        cb = D_FF + c * FF_CHUNK
        ya = conv(_dot(hb, wup_ref[:, ca:ca + FF_CHUNK]), bufs[2 * (c % 2)], c, ca)
        yb = conv(_dot(hb, wup_ref[:, cb:cb + FF_CHUNK]), bufs[2 * (c % 2) + 1], n_chunks + c, cb)
        act_ref[:, ca:ca + FF_CHUNK] = (ya * _sigmoid(ya) * yb).astype(BF16)
    o_ref[0] = _rms(x + _dot(act_ref[...], wdn_ref[...]), gn_ref[...])


def _rope_tables(seq, d, tm):
    inv = ROPE_THETA ** (-jnp.arange(0, d, 2, dtype=F32) / d)
    lanes = lambda t: jnp.tile(jnp.concatenate([t, t], axis=-1), (1, LANES // d))
    row_ang = jnp.arange(tm).astype(F32)[:, None] * inv[None, :]
    tile_ang = (jnp.arange(seq // tm) * tm).astype(F32)[:, None] * inv[None, :]
    half = jnp.ones((1, d // 2), F32)
    sign = jnp.tile(jnp.concatenate([-half, half], axis=-1), (1, LANES // d))
    return [lanes(jnp.cos(row_ang)), lanes(jnp.sin(row_ang)),
            lanes(jnp.cos(tile_ang)), lanes(jnp.sin(tile_ang)), sign]


def _params(n_axes):
    return pltpu.CompilerParams(dimension_semantics=("arbitrary",) * n_axes,
                                vmem_limit_bytes=VMEM_LIMIT)


def _layer(x, attn_norm, w_in, b_gate, q_norm, w_uq, kv_norm, w_ukv, w_o_mla, w_o_moba,
           w_out, ffn_norm, w_up, conv_w, conv_b, w_down, out_gain):
    B, S, D = x.shape
    H = MLA_HEADS
    T = ATT_TILE
    assert D == D_MODEL and S % T == 0 and S // T >= 3
    assert D_FF % FF_CHUNK == 0 and TM_PROJ % MOBA_BLOCK == 0 and T % TM_PROJ == 0
    nq = S // T
    nb = S // MOBA_BLOCK
    assert nb <= LANES and nb % SUBLANES == 0

    wcols = lambda a, b: w_in[:, a:b].astype(BF16)
    wq = wcols(OFF_QLAT, OFF_KVLAT)
    wkv = wcols(OFF_KVLAT, OFF_KPE)
    wpe = jnp.pad(wcols(OFF_KPE, OFF_MOBA), ((0, 0), (0, LANES - MLA_ROPE)))
    hd = MOBA_HEADS * MOBA_HD
    wmq = wcols(OFF_MOBA, OFF_MOBA + hd)
    wmk = wcols(OFF_MOBA + hd, OFF_MOBA + 2 * hd)
    wmv = wcols(OFF_MOBA + 2 * hd, OFF_GATE).T
    wg = wcols(OFF_GATE, OFF_GATE + N_BRANCH * D_MODEL)
    wuq3 = w_uq.astype(BF16).reshape(Q_LORA, H, MLA_NOPE + MLA_ROPE)
    wuqn = wuq3[:, :, :MLA_NOPE].reshape(Q_LORA, H * MLA_NOPE)
    wuqr = wuq3[:, :, MLA_NOPE:].reshape(Q_LORA, H * MLA_ROPE)
    wukv3 = w_ukv.astype(BF16).reshape(KV_LORA, H, MLA_NOPE + MLA_V)
    wuk = wukv3[:, :, :MLA_NOPE].reshape(KV_LORA, H * MLA_NOPE)
    wuv = wukv3[:, :, MLA_NOPE:].reshape(KV_LORA, H * MLA_V).T
    row2 = lambda v: v.reshape(1, -1)
    tm = TM_PROJ
    ropes = _rope_tables(S, MLA_ROPE, tm) + _rope_tables(S, MOBA_HD, tm)
    H2 = N_BRANCH * H
    head_out = lambda w, dt: jax.ShapeDtypeStruct((B, H2, S, w), dt)
    head_spec = lambda w: pl.BlockSpec((1, H2, tm, w), lambda b, s: (b, 0, s, 0))
    sub = T // tm
    assert MLA_V == LANES and MOBA_HD == LANES and MOBA_HEADS == H
    vt_out = jax.ShapeDtypeStruct((B, H2, nq, DV_AUG, T), BF16)
    vt_spec = pl.BlockSpec((1, H2, 1, DV_AUG, tm), lambda b, s: (b, 0, s // sub, 0, s % sub))
    weights = [row2(attn_norm), wq, wkv, wpe, wmq, wmk, wmv, wg, row2(b_gate), row2(q_norm),
               wuqn, wuqr, row2(kv_norm), wuk, wuv]
    q_all, k_all, vt_all, g = pl.pallas_call(
        _proj_kernel,
        grid=(B, S // tm),
        in_specs=[pl.BlockSpec((1, tm, D), lambda b, s: (b, s, 0))]
        + [_const_spec(w.shape) for w in weights + ropes],
        out_specs=[head_spec(ATT_QK), head_spec(ATT_QK), vt_spec,
                   pl.BlockSpec((1, tm, N_BRANCH * D), lambda b, s: (b, s, 0))],
        out_shape=[head_out(ATT_QK, BF16), head_out(ATT_QK, BF16), vt_out,
                   jax.ShapeDtypeStruct((B, S, N_BRANCH * D), BF16)],
        scratch_shapes=[pltpu.VMEM((H, nb, LANES), F32)],
        compiler_params=_params(2),
        name="proj",
    )(x, *weights, *ropes)

    seq_spec = lambda w, **kw: pl.BlockSpec((1, 1, S, w), lambda b, h: (b, h, 0, 0), **kw)
    vt_in = lambda w: pl.BlockSpec((1, 1, nq, DV_AUG, T), lambda b, h: (b, h, 0, 0, 0))
    att_out = pl.BlockSpec((1, S, LANES), lambda b, h: (b, 0, h))
    y = pl.pallas_call(
        functools.partial(_attn_kernel, nq=nq, T=T),
        grid=(B, H2),
        in_specs=[seq_spec(ATT_QK), seq_spec(ATT_QK), vt_in(LANES)],
        out_specs=att_out,
        out_shape=jax.ShapeDtypeStruct((B, S, H2 * LANES), BF16),
        scratch_shapes=_flash_scratch(nq, T, DV_AUG),
        compiler_params=_params(2),
        name="attn",
    )(q_all, k_all, vt_all)

    tm = TM_FFN
    n_chunks = D_FF // FF_CHUNK
    rows = lambda w: pl.BlockSpec((1, tm, w), lambda b, s: (b, s, 0))
    ffn_w = [w_o_mla.astype(BF16), w_o_moba.astype(BF16), w_out.astype(BF16),
             row2(ffn_norm), w_up.astype(BF16), conv_w, row2(conv_b), w_down.astype(BF16),
             row2(out_gain)]
    out = pl.pallas_call(
        functools.partial(_ffn_kernel, tm=tm, n_chunks=n_chunks),
        grid=(B, S // tm),
        in_specs=[rows(D), rows(D), pl.BlockSpec((1, tm, D), lambda b, s: (b, s, 1)),
                  rows(N_BRANCH * D)]
        + [_const_spec(w.shape) for w in ffn_w],
        out_specs=pl.BlockSpec((1, tm, D), lambda b, s: (b, s, 0)),
        out_shape=jax.ShapeDtypeStruct((B, S, D), F32),
        scratch_shapes=[pltpu.VMEM((2 * n_chunks, SUBLANES, FF_CHUNK), F32),
                        pltpu.VMEM((tm, D_FF), BF16)]
        + [pltpu.VMEM((tm + SUBLANES, FF_CHUNK), F32)] * 4,
        compiler_params=_params(2),
        name="ffn",
    )(x, y, y, g, *ffn_w)
    return out


def kernel(x, attn_norm, w_in, b_gate, q_norm, w_uq, kv_norm, w_ukv, w_o_mla, w_o_moba,
           w_out, ffn_norm, w_up, conv_w, conv_b, w_down, final_norm):
    depth = w_in.shape[0]
    assert depth == 1
    return _layer(x, attn_norm[0], w_in[0], b_gate[0], q_norm[0], w_uq[0], kv_norm[0],
                  w_ukv[0], w_o_mla[0], w_o_moba[0], w_out[0], ffn_norm[0], w_up[0],
                  conv_w[0], conv_b[0], w_down[0], final_norm)
```
